```python
import jax
import jax.numpy as jnp
from jax import lax
import numpy as np

D_MODEL = 1024
BATCH = 2
SEQ = 16384
DEPTH = 2

CHUNK = 64
Q_BLOCK = 128
EPS = 1e-6
ROPE_BASE = 10000.0
RET_HEADS = 4
RET_DK = 128
RET_DV = 256
RET_QK = RET_HEADS * RET_DK
RET_V = RET_HEADS * RET_DV
LRU_WIDTH = D_MODEL
LRU_BLOCKS = 8
LRU_BW = LRU_WIDTH // LRU_BLOCKS
CONV_W = 4
LRU_C = 8.0
FOX_HEADS = 8
FOX_DH = 128
FOX_W = FOX_HEADS * FOX_DH
D_FF = 4 * D_MODEL
N_BRANCH = 3
SPLITS = (RET_QK, RET_QK, RET_V, RET_V, LRU_WIDTH, LRU_WIDTH, FOX_W, FOX_W, FOX_W, FOX_HEADS, N_BRANCH * D_MODEL)
D_IN = 2 * RET_QK + 2 * RET_V + 2 * LRU_WIDTH + 3 * FOX_W + FOX_HEADS + N_BRANCH * D_MODEL

kernel_name = 'chunk_causal_hybrid_retention_rglru_fox_block'


def rms_norm(x, g):
    xf = x.astype(jnp.float32)
    y = xf * lax.rsqrt(jnp.mean(xf * xf, axis=-1, keepdims=True) + EPS)
    return (y * g.astype(jnp.float32)).astype(x.dtype)


def head_norm(o):
    mu = jnp.mean(o, axis=-1, keepdims=True)
    var = jnp.mean(jnp.square(o - mu), axis=-1, keepdims=True)
    return (o - mu) * lax.rsqrt(var + EPS)


def rope(x, pos):
    half = x.shape[-1] // 2
    inv = ROPE_BASE ** (-jnp.arange(half, dtype=jnp.float32) / half)
    ang = pos.astype(jnp.float32)[:, None] * inv[None, :]
    cos = jnp.cos(ang)[None, :, None, :]
    sin = jnp.sin(ang)[None, :, None, :]
    xf = x.astype(jnp.float32)
    x1, x2 = xf[..., :half], xf[..., half:]
    return jnp.concatenate([x1 * cos - x2 * sin, x1 * sin + x2 * cos], axis=-1)


def retention(q, k, v):
    B, S, H, dk = q.shape
    nc = S // CHUNK
    log_g = jnp.log(1.0 - 2.0 ** (-5.0 - jnp.arange(H, dtype=jnp.float32)))
    idx = jnp.arange(CHUNK, dtype=jnp.float32)
    inner = jnp.exp(jnp.abs(idx[:, None] - idx[None, :])[None] * log_g[:, None, None])
    q_dec = jnp.exp((idx + 1.0)[None, :] * log_g[:, None])[None, :, :, None]
    k_dec = jnp.exp((CHUNK - 1.0 - idx)[None, :] * log_g[:, None])[None, :, :, None]
    chunk_dec = jnp.exp(CHUNK * log_g)[None, :, None, None]

    def to_chunks(t):
        return t.astype(jnp.float32).reshape(B, nc, CHUNK, H, t.shape[-1]).transpose(1, 0, 3, 2, 4)

    qc, kc, vc = to_chunks(q), to_chunks(k * RET_DK ** -0.5), to_chunks(v)

    def step(state, xs):
        qi, ki, vi = xs
        s = jnp.einsum('bhqd,bhkd->bhqk', qi, ki) * inner[None]
        o = jnp.einsum('bhqk,bhkv->bhqv', s, vi) + jnp.einsum('bhqd,bhdv->bhqv', qi * q_dec, state)
        state = state * chunk_dec + jnp.einsum('bhkd,bhkv->bhdv', ki * k_dec, vi)
        return state, o

    state0 = jnp.zeros((B, H, dk, v.shape[-1]), jnp.float32)
    _, o = lax.scan(step, state0, (qc, kc, vc))
    return o.transpose(1, 0, 3, 2, 4).reshape(B, S, H, v.shape[-1])


def causal_conv(x, w, b):
    S = x.shape[1]
    xp = jnp.pad(x, ((0, 0), (CONV_W - 1, 0), (0, 0)))
    out = xp[:, 0:S] * w[0]
    for i in range(1, CONV_W):
        out = out + xp[:, i:i + S] * w[i]
    return out + b


def rg_lru(x, w_a, b_a, w_x, b_x, lam):
    B, S, W = x.shape
    xb = x.reshape(B, S, LRU_BLOCKS, LRU_BW)
    r = jax.nn.sigmoid(jnp.einsum('bsni,nio->bsno', xb, w_a).reshape(B, S, W) + b_a)
    i = jax.nn.sigmoid(jnp.einsum('bsni,nio->bsno', xb, w_x).reshape(B, S, W) + b_x)
    log_a = -LRU_C * r.astype(jnp.float32) * jax.nn.softplus(-lam.astype(jnp.float32))
    a = jnp.exp(log_a)
    u = jnp.sqrt(jnp.maximum(-jnp.expm1(2.0 * log_a), 0.0)) * (i * x).astype(jnp.float32)

    def combine(left, right):
        a1, b1 = left
        a2, b2 = right
        return a1 * a2, a2 * b1 + b2

    _, h = lax.associative_scan(combine, (a, u), axis=1)
    return h.astype(x.dtype)


def forgetting_attention(q, k, v, log_f):
    B, S, H, D = q.shape
    nb = S // Q_BLOCK
    cum = jnp.cumsum(log_f, axis=1).transpose(0, 2, 1)
    kf = k.astype(jnp.float32)
    vf = v.astype(jnp.float32)
    qb = q.astype(jnp.float32).reshape(B, nb, Q_BLOCK, H, D).transpose(1, 0, 2, 3, 4)
    cq = cum.reshape(B, H, nb, Q_BLOCK).transpose(2, 0, 1, 3)
    kpos = jnp.arange(S)
    scale = D ** -0.5

    def block(args):
        qi, ci, start = args
        s = jnp.einsum('bqhd,bkhd->bhqk', qi, kf) * scale + (ci[..., None] - cum[:, :, None, :])
        qpos = start + jnp.arange(Q_BLOCK)
        s = jnp.where((kpos[None, :] <= qpos[:, None])[None, None], s, -jnp.inf)
        p = jax.nn.softmax(s, axis=-1)
        return jnp.einsum('bhqk,bkhd->bqhd', p, vf)

    o = lax.map(block, (qb, cq, jnp.arange(nb) * Q_BLOCK))
    return o.transpose(1, 0, 2, 3, 4).reshape(B, S, H, D)


def hybrid_layer(x, c, ada_w, ada_b, g_pre_mix, g_post_mix, g_pre_mlp, g_post_mlp, w_in, conv_w, conv_b,
                 w_a, b_a, w_x, b_x, lam, b_f, ret_w_o, lru_w_o, fox_w_o, w_out, w1, w2):
    B, S, _ = x.shape
    mod = jax.nn.silu(c) @ ada_w + ada_b
    sh_m, sc_m, gt_m, sh_f, sc_f, gt_f = [t[:, None, :] for t in jnp.split(mod, 6, axis=-1)]

    h = rms_norm(x, g_pre_mix) * (1.0 + sc_m) + sh_m
    proj = h @ w_in
    offs = [int(o) for o in np.cumsum(SPLITS)[:-1]]
    rq, rk, rv, rg, lx, ly, fq, fk, fv, ff, gates = jnp.split(proj, offs, axis=-1)
    pos = jnp.arange(S)

    rq = rope(rq.reshape(B, S, RET_HEADS, RET_DK), pos)
    rk = rope(rk.reshape(B, S, RET_HEADS, RET_DK), pos)
    ret = head_norm(retention(rq, rk, rv.reshape(B, S, RET_HEADS, RET_DV)))
    y_ret = jax.nn.silu(rg) * ret.reshape(B, S, RET_V).astype(x.dtype)

    u = causal_conv(lx, conv_w, conv_b)
    y_lru = jax.nn.gelu(ly) * rg_lru(u, w_a, b_a, w_x, b_x, lam)

    log_f = jax.nn.log_sigmoid((ff + b_f).astype(jnp.float32))
    fo = forgetting_attention(fq.reshape(B, S, FOX_HEADS, FOX_DH), fk.reshape(B, S, FOX_HEADS, FOX_DH),
                              fv.reshape(B, S, FOX_HEADS, FOX_DH), log_f)
    y_fox = fo.reshape(B, S, FOX_W).astype(x.dtype)

    g = jax.nn.sigmoid(gates).reshape(B, S, N_BRANCH, D_MODEL)
    merged = g[:, :, 0] * (y_ret @ ret_w_o) + g[:, :, 1] * (y_lru @ lru_w_o) + g[:, :, 2] * (y_fox @ fox_w_o)
    x = x + gt_m * rms_norm(merged @ w_out, g_post_mix)

    h = rms_norm(x, g_pre_mlp) * (1.0 + sc_f) + sh_f
    f = jnp.square(jax.nn.relu(h @ w1)) @ w2
    return x + gt_f * rms_norm(f, g_post_mlp)


def setup_inputs(seed: int = 0) -> dict:
    key = jax.random.key(seed)
    ks = jax.random.split(key, 28)
    f32 = jnp.float32
    L, D = DEPTH, D_MODEL

    def nrm(k, shape, scale):
        return jax.random.normal(k, shape, f32) * scale

    a0 = jax.random.uniform(ks[14], (L, LRU_WIDTH), f32, 0.9, 0.999)
    s = a0 ** (1.0 / LRU_C)
    lam = jnp.log(s) - jnp.log1p(-s)
    return {
        'x': nrm(ks[0], (BATCH, SEQ, D), 1.0),
        'c': nrm(ks[1], (BATCH, D), 1.0),
        'ada_w': nrm(ks[2], (L, D, 6 * D), D ** -0.5),
        'ada_b': nrm(ks[3], (L, 6 * D), 0.02),
        'norm_pre_mix': 1.0 + nrm(ks[4], (L, D), 0.05),
        'norm_post_mix': 1.0 + nrm(ks[5], (L, D), 0.05),
        'norm_pre_mlp': 1.0 + nrm(ks[6], (L, D), 0.05),
        'norm_post_mlp': 1.0 + nrm(ks[7], (L, D), 0.05),
        'w_in': nrm(ks[8], (L, D, D_IN), D ** -0.5),
        'conv_w': nrm(ks[9], (L, CONV_W, LRU_WIDTH), CONV_W ** -0.5),
        'conv_b': nrm(ks[10], (L, LRU_WIDTH), 0.02),
        'lru_w_a': nrm(ks[11], (L, LRU_BLOCKS, LRU_BW, LRU_BW), LRU_BW ** -0.5),
        'lru_b_a': nrm(ks[12], (L, LRU_WIDTH), 0.02),
        'lru_w_x': nrm(ks[13], (L, LRU_BLOCKS, LRU_BW, LRU_BW), LRU_BW ** -0.5),
        'lru_b_x': nrm(ks[15], (L, LRU_WIDTH), 0.02),
        'lru_lambda': lam,
        'fox_b_f': jax.random.uniform(ks[16], (L, FOX_HEADS), f32, 1.0, 5.0),
        'ret_w_o': nrm(ks[17], (L, RET_V, D), RET_V ** -0.5),
        'lru_w_o': nrm(ks[18], (L, LRU_WIDTH, D), LRU_WIDTH ** -0.5),
        'fox_w_o': nrm(ks[19], (L, FOX_W, D), FOX_W ** -0.5),
        'w_out': nrm(ks[20], (L, D, D), D ** -0.5),
        'mlp_w1': nrm(ks[21], (L, D, D_FF), D ** -0.5),
        'mlp_w2': nrm(ks[22], (L, D_FF, D), D_FF ** -0.5),
    }


def reference(x, c, ada_w, ada_b, norm_pre_mix, norm_post_mix, norm_pre_mlp, norm_post_mlp, w_in, conv_w,
              conv_b, lru_w_a, lru_b_a, lru_w_x, lru_b_x, lru_lambda, fox_b_f, ret_w_o, lru_w_o, fox_w_o,
              w_out, mlp_w1, mlp_w2):
    for l in range(DEPTH):
        x = hybrid_layer(x, c, ada_w[l], ada_b[l], norm_pre_mix[l], norm_post_mix[l], norm_pre_mlp[l],
                         norm_post_mlp[l], w_in[l], conv_w[l], conv_b[l], lru_w_a[l], lru_b_a[l], lru_w_x[l],
                         lru_b_x[l], lru_lambda[l], fox_b_f[l], ret_w_o[l], lru_w_o[l], fox_w_o[l], w_out[l],
                         mlp_w1[l], mlp_w2[l])
    return x
```

```python
import functools

import jax
import jax.numpy as jnp
from jax import lax
from jax.experimental import pallas as pl
from jax.experimental.pallas import tpu as pltpu

F32 = jnp.float32
BF16 = jnp.bfloat16

D_MODEL = 1024
EPS = 1e-6
ROPE_BASE = 10000.0
CHUNK = 64
RET_HEADS, RET_DK, RET_DV = 4, 128, 256
LRU_BLOCKS, LRU_BW, CONV_W, LRU_C = 8, 128, 4, 8.0
FOX_HEADS, FOX_DH = 8, 128
D_FF = 4 * D_MODEL
LANE = 128

G_RQ, G_RK, G_RV, G_RG, G_LX, G_LY, G_FQ, G_FK, G_FV, G_GATE = 0, 4, 8, 16, 24, 32, 40, 48, 56, 64
N_GROUPS = 88
FF_COL = 2 * RET_HEADS * RET_DK + 2 * RET_HEADS * RET_DV + 2 * D_MODEL + 3 * FOX_HEADS * FOX_DH

NEG = -1e30
VMEM_LIMIT = 56 * 1024 * 1024


def _cparams(sem):
    return pltpu.CompilerParams(dimension_semantics=sem, vmem_limit_bytes=VMEM_LIMIT)


def _sigmoid(z):
    return 1.0 / (1.0 + jnp.exp(-z))


def _softplus(z):
    return jnp.maximum(z, 0.0) + jnp.log1p(jnp.exp(-jnp.abs(z)))


def _rms(x, g):
    return x * lax.rsqrt(jnp.mean(x * x, axis=-1, keepdims=True) + EPS) * g


def _cat(ref, lo, n):
    return jnp.concatenate([ref[lo + k] for k in range(n)], axis=-1)


def _mod_kernel(c_ref, w_ref, b_ref, o_ref):
    c = c_ref[...]
    o_ref[0] = jnp.dot(c * _sigmoid(c), w_ref[0], preferred_element_type=F32,
                       precision=lax.Precision.HIGHEST) + b_ref[0]


def _modulation(c, ada_w, ada_b):
    L, D, D6 = ada_w.shape
    B = c.shape[0]
    c8 = jnp.zeros((8, D), F32).at[:B].set(c)
    tn = 1024
    out = pl.pallas_call(
        _mod_kernel,
        grid=(L, D6 // tn),
        in_specs=[
            pl.BlockSpec((8, D), lambda l, j: (0, 0)),
            pl.BlockSpec((1, D, tn), lambda l, j: (l, 0, j)),
            pl.BlockSpec((1, 1, tn), lambda l, j: (l, 0, j)),
        ],
        out_specs=pl.BlockSpec((1, 8, tn), lambda l, j: (l, 0, j)),
        out_shape=jax.ShapeDtypeStruct((L, 8, D6), F32),
        compiler_params=_cparams(("parallel", "parallel")),
        name="adaln_mod",
    )(c8, ada_w, ada_b.reshape(L, 1, D6))
    return out[:, :B].reshape(L, B, 1, D6)


def _inproj_kernel(x_ref, mod_ref, g_ref, w_ref, wff_ref, p_ref, ff_ref, h_scr):
    D = D_MODEL

    @pl.when(pl.program_id(1) == 0)
    def _():
        h = _rms(x_ref[...], g_ref[...]) * (1.0 + mod_ref[:, D:2 * D]) + mod_ref[:, 0:D]
        hb = h.astype(BF16)
        h_scr[...] = hb
        ff_ref[...] = jnp.dot(hb, wff_ref[...], preferred_element_type=F32)

    acc = jnp.dot(h_scr[...], w_ref[...], preferred_element_type=F32)
    for g in range(acc.shape[1] // LANE):
        p_ref[g] = acc[:, g * LANE:(g + 1) * LANE].astype(BF16)


def _in_projection(x2, mod_l, g_pre, w_main, w_ff, S, tm=1024, tn=1024):
    T, D = x2.shape
    n_main = w_main.shape[1]
    nt = S // tm
    return pl.pallas_call(
        _inproj_kernel,
        grid=(T // tm, n_main // tn),
        in_specs=[
            pl.BlockSpec((tm, D), lambda i, j: (i, 0)),
            pl.BlockSpec((None, 1, 6 * D), lambda i, j: (i // nt, 0, 0)),
            pl.BlockSpec((1, D), lambda i, j: (0, 0)),
            pl.BlockSpec((D, tn), lambda i, j: (0, j)),
            pl.BlockSpec((D, LANE), lambda i, j: (0, 0)),
        ],
        out_specs=[
            pl.BlockSpec((tn // LANE, tm, LANE), lambda i, j: (j, i, 0)),
            pl.BlockSpec((tm, LANE), lambda i, j: (i, 0)),
        ],
        out_shape=[
            jax.ShapeDtypeStruct((n_main // LANE, T, LANE), BF16),
            jax.ShapeDtypeStruct((T, LANE), F32),
        ],
        scratch_shapes=[pltpu.VMEM((tm, D), BF16)],
        compiler_params=_cparams(("parallel", "arbitrary")),
        name="in_proj",
    )(x2, mod_l, g_pre.reshape(1, D), w_main, w_ff)


def _ret_kernel(q_ref, k_ref, v_ref, rg_ref, cos_ref, sin_ref, dmat_ref, qdec_ref, kdec_ref, cdec_ref,
                o_ref, state):
    @pl.when(pl.program_id(1) == 0)
    def _():
        state[...] = jnp.zeros_like(state)

    cos = cos_ref[...]
    sin = sin_ref[...]
    half = RET_DK // 2
    for h in range(RET_HEADS):
        q = q_ref[h].astype(F32)
        k = k_ref[h].astype(F32)
        qr = q * cos + pltpu.roll(q, half, 1) * sin
        kr = k * cos + pltpu.roll(k, half, 1) * sin
        v = _cat(v_ref, 2 * h, 2)
        s = lax.dot_general(qr.astype(BF16), kr.astype(BF16), (((1,), (1,)), ((), ())),
                            preferred_element_type=F32) * dmat_ref[h]
        st = state[h]
        o = jnp.dot(s.astype(BF16), v, preferred_element_type=F32)
        o = o + jnp.dot((qr * qdec_ref[h]).astype(BF16), st.astype(BF16), preferred_element_type=F32)
        kd = (kr * kdec_ref[h]).astype(BF16)
        state[h] = st * cdec_ref[h] + lax.dot_general(kd, v, (((0,), (0,)), ((), ())),
                                                      preferred_element_type=F32)
        d = o - jnp.mean(o, axis=-1, keepdims=True)
        on = d * lax.rsqrt(jnp.mean(d * d, axis=-1, keepdims=True) + EPS)
        g = _cat(rg_ref, 2 * h, 2).astype(F32)
        o_ref[:, h * RET_DV:(h + 1) * RET_DV] = (g * _sigmoid(g) * on).astype(BF16)


def _retention_tables(S, bt):
    H = RET_HEADS
    log_g = jnp.log(1.0 - 2.0 ** (-5.0 - jnp.arange(H, dtype=F32)))
    idx = jnp.arange(bt, dtype=F32)
    dist = jnp.abs(idx[:, None] - idx[None, :])
    ch = jnp.arange(bt) // CHUNK
    allowed = ch[None, :] <= ch[:, None]
    dmat = jnp.where(allowed[None], jnp.exp(dist[None] * log_g[:, None, None]), 0.0)
    qdec = jnp.exp((idx + 1.0)[None, :] * log_g[:, None])
    kdec = jnp.exp((bt - 1.0 - idx)[None, :] * log_g[:, None])
    cdec = jnp.exp(bt * log_g)
    qdec = jnp.broadcast_to(qdec[:, :, None], (H, bt, LANE))
    kdec = jnp.broadcast_to(kdec[:, :, None], (H, bt, LANE))
    cdec = jnp.broadcast_to(cdec[:, None, None], (H, 1, RET_DV))
    half = RET_DK // 2
    inv = ROPE_BASE ** (-jnp.arange(half, dtype=F32) / half)
    ang = jnp.arange(S).astype(F32)[:, None] * inv[None, :]
    cos, sin = jnp.cos(ang), jnp.sin(ang)
    cos2 = jnp.concatenate([cos, cos], axis=-1)
    sin2 = jnp.concatenate([-sin, sin], axis=-1)
    return cos2, sin2, dmat, qdec, kdec, cdec


def _retention(P, tables, B, S, bt):
    T = B * S
    nt = S // bt
    cos2, sin2, dmat, qdec, kdec, cdec = tables
    H = RET_HEADS
    tok = lambda b, i: b * nt + i
    return pl.pallas_call(
        _ret_kernel,
        grid=(B, nt),
        in_specs=[
            pl.BlockSpec((4, bt, LANE), lambda b, i: (G_RQ // 4, tok(b, i), 0)),
            pl.BlockSpec((4, bt, LANE), lambda b, i: (G_RK // 4, tok(b, i), 0)),
            pl.BlockSpec((8, bt, LANE), lambda b, i: (G_RV // 8, tok(b, i), 0)),
            pl.BlockSpec((8, bt, LANE), lambda b, i: (G_RG // 8, tok(b, i), 0)),
            pl.BlockSpec((bt, LANE), lambda b, i: (i, 0)),
            pl.BlockSpec((bt, LANE), lambda b, i: (i, 0)),
            pl.BlockSpec((H, bt, bt), lambda b, i: (0, 0, 0)),
            pl.BlockSpec((H, bt, LANE), lambda b, i: (0, 0, 0)),
            pl.BlockSpec((H, bt, LANE), lambda b, i: (0, 0, 0)),
            pl.BlockSpec((H, 1, RET_DV), lambda b, i: (0, 0, 0)),
        ],
        out_specs=pl.BlockSpec((bt, H * RET_DV), lambda b, i: (tok(b, i), 0)),
        out_shape=jax.ShapeDtypeStruct((T, H * RET_DV), BF16),
        scratch_shapes=[pltpu.VMEM((H, RET_DK, RET_DV), F32)],
        compiler_params=_cparams(("parallel", "arbitrary")),
        name="retention",
    )(P, P, P, P, cos2, sin2, dmat, qdec, kdec, cdec)


def _lru_kernel(lx_ref, ly_ref, cw_ref, cb_ref, wax_ref, ba_ref, bx_ref, lam_ref, o_ref,
                tail, hprev, a_scr, b_scr, h_scr):
    bt = lx_ref.shape[1]

    @pl.when(pl.program_id(1) == 0)
    def _():
        tail[...] = jnp.zeros_like(tail)
        hprev[...] = jnp.zeros_like(hprev)

    r8 = lax.broadcasted_iota(jnp.int32, (bt, LANE), 0) & 7
    for n in range(LRU_BLOCKS):
        sl = slice(n * LANE, (n + 1) * LANE)
        x = lx_ref[n].astype(F32)
        xp = jnp.concatenate([tail[:, sl], x], axis=0)
        u = cb_ref[:, sl] + xp[8:8 + bt] * cw_ref[3:4, sl]
        for i in range(CONV_W - 1):
            u = u + xp[5 + i:5 + i + bt] * cw_ref[i:i + 1, sl]
        tail[:, sl] = x[bt - 8:bt]
        gates = jnp.dot(u.astype(BF16), wax_ref[n], preferred_element_type=F32)
        r = _sigmoid(gates[:, :LANE] + ba_ref[:, sl])
        ig = _sigmoid(gates[:, LANE:] + bx_ref[:, sl])
        log_a = (-LRU_C) * r * _softplus(-lam_ref[:, sl])
        a = jnp.exp(log_a)
        bb = jnp.sqrt(jnp.maximum(1.0 - a * a, 0.0)) * (ig * u)
        for d in (1, 2, 4):
            m = r8 >= d
            a_sh = jnp.where(m, pltpu.roll(a, d, 0), 1.0)
            b_sh = jnp.where(m, pltpu.roll(bb, d, 0), 0.0)
            bb = a * b_sh + bb
            a = a * a_sh
        a_scr[:, sl] = a
        b_scr[:, sl] = bb

    def body(c, hp):
        r0 = pl.multiple_of(c * 8, 8)
        h = a_scr[pl.ds(r0, 8), :] * hp + b_scr[pl.ds(r0, 8), :]
        h_scr[pl.ds(r0, 8), :] = h
        return jnp.broadcast_to(h[7:8, :], h.shape)

    hprev[...] = lax.fori_loop(0, bt // 8, body, hprev[...], unroll=4)
    for n in range(LRU_BLOCKS):
        sl = slice(n * LANE, (n + 1) * LANE)
        o_ref[:, sl] = (jax.nn.gelu(ly_ref[n].astype(F32)) * h_scr[:, sl]).astype(BF16)


def _rg_lru(P, conv_w, conv_b, w_ax, b_a, b_x, lam, B, S, bt):
    T = B * S
    nt = S // bt
    W = LRU_BLOCKS * LRU_BW
    tok = lambda b, i: b * nt + i
    full = lambda shape: pl.BlockSpec(shape, lambda b, i: (0,) * len(shape))
    return pl.pallas_call(
        _lru_kernel,
        grid=(B, nt),
        in_specs=[
            pl.BlockSpec((8, bt, LANE), lambda b, i: (G_LX // 8, tok(b, i), 0)),
            pl.BlockSpec((8, bt, LANE), lambda b, i: (G_LY // 8, tok(b, i), 0)),
            full((CONV_W, W)), full((1, W)), full((LRU_BLOCKS, LRU_BW, 2 * LRU_BW)),
            full((1, W)), full((1, W)), full((1, W)),
        ],
        out_specs=pl.BlockSpec((bt, W), lambda b, i: (tok(b, i), 0)),
        out_shape=jax.ShapeDtypeStruct((T, W), BF16),
        scratch_shapes=[pltpu.VMEM((8, W), F32), pltpu.VMEM((8, W), F32),
                        pltpu.VMEM((bt, W), F32), pltpu.VMEM((bt, W), F32), pltpu.VMEM((bt, W), F32)],
        compiler_params=_cparams(("parallel", "arbitrary")),
        name="rg_lru",
    )(P, P, conv_w, conv_b.reshape(1, W), w_ax, b_a.reshape(1, W), b_x.reshape(1, W), lam.reshape(1, W))


def _cum_kernel(ff_ref, bf_ref, tri_ref, aq_ref, ak_ref, carry):
    bc = ff_ref.shape[0]

    @pl.when(pl.program_id(1) == 0)
    def _():
        carry[...] = jnp.zeros_like(carry)

    z = ff_ref[...] + bf_ref[...]
    logf = jnp.minimum(z, 0.0) - jnp.log1p(jnp.exp(-jnp.abs(z)))
    cum = jnp.dot(tri_ref[...], logf, preferred_element_type=F32, precision=lax.Precision.HIGHEST) + carry[...]
    carry[...] = cum[bc - 1:bc]
    lane = lax.broadcasted_iota(jnp.int32, (bc, LANE), 1)
    for h in range(FOX_HEADS):
        c = jnp.broadcast_to(cum[:, h:h + 1], (bc, LANE))
        hi = c.astype(BF16).astype(F32)
        r1 = c - hi
        mid = r1.astype(BF16).astype(F32)
        lo = r1 - mid
        aq = jnp.where(lane == 0, hi, jnp.where(lane == 1, mid, jnp.where(lane == 2, lo,
                       jnp.where(lane < 6, 1.0, 0.0))))
        ak = jnp.where(lane < 3, 1.0, jnp.where(lane == 3, -hi, jnp.where(lane == 4, -mid,
                       jnp.where(lane == 5, -lo, 0.0))))
        aq_ref[h] = aq.astype(BF16)
        ak_ref[h] = ak.astype(BF16)


def _forget_cumsum(ff, b_f, B, S, bc=256):
    T = B * S
    nt = S // bc
    bf = jnp.zeros((1, LANE), F32).at[0, :FOX_HEADS].set(b_f)
    tri = (jnp.arange(bc)[None, :] <= jnp.arange(bc)[:, None]).astype(F32)
    tok = lambda b, i: b * nt + i
    out = jax.ShapeDtypeStruct((FOX_HEADS, T, LANE), BF16)
    return pl.pallas_call(
        _cum_kernel,
        grid=(B, nt),
        in_specs=[
            pl.BlockSpec((bc, LANE), lambda b, i: (tok(b, i), 0)),
            pl.BlockSpec((1, LANE), lambda b, i: (0, 0)),
            pl.BlockSpec((bc, bc), lambda b, i: (0, 0)),
        ],
        out_specs=[pl.BlockSpec((FOX_HEADS, bc, LANE), lambda b, i: (0, tok(b, i), 0))] * 2,
        out_shape=[out, out],
        scratch_shapes=[pltpu.VMEM((1, LANE), F32)],
        compiler_params=_cparams(("parallel", "arbitrary")),
        name="forget_cumsum",
    )(ff, bf, tri)


def _fox_kernel(q_ref, aq_ref, k_ref, ak_ref, v_ref, o_ref, m_scr, acc_scr):
    tq = q_ref.shape[1]
    i = pl.program_id(1)
    qa = jnp.concatenate([q_ref[0], aq_ref[0]], axis=-1)
    m_scr[...] = jnp.full_like(m_scr, NEG)
    acc_scr[...] = jnp.zeros_like(acc_scr)
    ones = jnp.ones((tq, LANE), BF16)

    def step(j, masked):
        r0 = pl.multiple_of(j * tq, tq)
        ka = jnp.concatenate([k_ref[0, pl.ds(r0, tq), :], ak_ref[0, pl.ds(r0, tq), :]], axis=-1)
        va = jnp.concatenate([v_ref[0, pl.ds(r0, tq), :], ones], axis=-1)
        s = lax.dot_general(qa, ka, (((1,), (1,)), ((), ())), preferred_element_type=F32)
        if masked:
            row = lax.broadcasted_iota(jnp.int32, s.shape, 0)
            col = lax.broadcasted_iota(jnp.int32, s.shape, 1)
            s = jnp.where(col <= row, s, NEG)
        m_prev = m_scr[...]
        m_new = jnp.maximum(m_prev, jnp.max(s, axis=1, keepdims=True))
        p = jnp.exp(s - m_new[:, 0:1])
        alpha = jnp.exp(m_prev - m_new)
        acc_scr[...] = acc_scr[...] * jnp.concatenate([alpha, alpha], axis=-1) + jnp.dot(
            p.astype(BF16), va, preferred_element_type=F32)
        m_scr[...] = m_new

    def body(j, carry):
        step(j, False)
        return carry

    lax.fori_loop(0, i, body, 0)
    step(i, True)
    acc = acc_scr[...]
    o_ref[...] = (acc[:, :LANE] / acc[:, LANE:]).astype(BF16)


def _fox_attention(P, Aq, Ak, B, S, tq):
    T = B * S
    nq = S // tq
    H = FOX_HEADS
    return pl.pallas_call(
        _fox_kernel,
        grid=(B * H, nq),
        in_specs=[
            pl.BlockSpec((1, tq, LANE), lambda g, i: (G_FQ + g % H, (g // H) * nq + i, 0)),
            pl.BlockSpec((1, tq, LANE), lambda g, i: (g % H, (g // H) * nq + i, 0)),
            pl.BlockSpec((1, S, LANE), lambda g, i: (G_FK + g % H, g // H, 0)),
            pl.BlockSpec((1, S, LANE), lambda g, i: (g % H, g // H, 0)),
            pl.BlockSpec((1, S, LANE), lambda g, i: (G_FV + g % H, g // H, 0)),
        ],
        out_specs=pl.BlockSpec((tq, LANE), lambda g, i: ((g // H) * nq + i, g % H)),
        out_shape=jax.ShapeDtypeStruct((T, H * FOX_DH), BF16),
        scratch_shapes=[pltpu.VMEM((tq, LANE), F32), pltpu.VMEM((tq, 2 * LANE), F32)],
        compiler_params=_cparams(("parallel", "arbitrary")),
        name="fox_attention",
    )(P, Aq, P, Ak, P)


def _merge_kernel(yr_ref, yl_ref, yf_ref, g0_ref, g1_ref, g2_ref, x_ref, mod_ref, gp_ref,
                  wr_ref, wl_ref, wf_ref, wo_ref, o_ref):
    D = D_MODEL

    def branch(y_ref, w_ref, g_ref):
        g = _sigmoid(_cat(g_ref, 0, 8).astype(F32))
        return g * jnp.dot(y_ref[...], w_ref[...], preferred_element_type=F32)

    m = branch(yr_ref, wr_ref, g0_ref) + branch(yl_ref, wl_ref, g1_ref) + branch(yf_ref, wf_ref, g2_ref)
    z = jnp.dot(m.astype(BF16), wo_ref[...], preferred_element_type=F32)
    o_ref[...] = x_ref[...] + mod_ref[:, 2 * D:3 * D] * _rms(z, gp_ref[...])


def _merge(y_ret, y_lru, y_fox, P, x2, mod_l, g_post, w_r, w_l, w_f, w_o, S, tm=512):
    T, D = x2.shape
    nt = S // tm
    row = pl.BlockSpec((tm, D), lambda i: (i, 0))
    wspec = pl.BlockSpec((D, D), lambda i: (0, 0))
    gate = lambda k: pl.BlockSpec((8, tm, LANE), lambda i: (G_GATE // 8 + k, i, 0))
    return pl.pallas_call(
        _merge_kernel,
        grid=(T // tm,),
        in_specs=[row, row, row, gate(0), gate(1), gate(2), row,
                  pl.BlockSpec((None, 1, 6 * D), lambda i: (i // nt, 0, 0)),
                  pl.BlockSpec((1, D), lambda i: (0, 0)),
                  wspec, wspec, wspec, wspec],
        out_specs=row,
        out_shape=jax.ShapeDtypeStruct((T, D), F32),
        compiler_params=_cparams(("parallel",)),
        name="merge_out_proj",
    )(y_ret, y_lru, y_fox, P, P, P, x2, mod_l, g_post.reshape(1, D), w_r, w_l, w_f, w_o)


def _mlp_kernel(x_ref, mod_ref, gpre_ref, gpost_ref, w1_ref, w2_ref, o_ref):
    D = D_MODEL
    x = x_ref[...]
    h = (_rms(x, gpre_ref[...]) * (1.0 + mod_ref[:, 4 * D:5 * D]) + mod_ref[:, 3 * D:4 * D]).astype(BF16)
    f = jnp.zeros(x.shape, F32)
    ck = 1024
    for c in range(D_FF // ck):
        a = jnp.maximum(jnp.dot(h, w1_ref[:, c * ck:(c + 1) * ck], preferred_element_type=F32), 0.0)
        f = f + jnp.dot((a * a).astype(BF16), w2_ref[c * ck:(c + 1) * ck, :], preferred_element_type=F32)
    o_ref[...] = x + mod_ref[:, 5 * D:6 * D] * _rms(f, gpost_ref[...])


def _mlp(x2, mod_l, g_pre, g_post, w1, w2, S, tm=512):
    T, D = x2.shape
    nt = S // tm
    row = pl.BlockSpec((tm, D), lambda i: (i, 0))
    vec = pl.BlockSpec((1, D), lambda i: (0, 0))
    return pl.pallas_call(
        _mlp_kernel,
        grid=(T // tm,),
        in_specs=[row, pl.BlockSpec((None, 1, 6 * D), lambda i: (i // nt, 0, 0)), vec, vec,
                  pl.BlockSpec((D, D_FF), lambda i: (0, 0)), pl.BlockSpec((D_FF, D), lambda i: (0, 0))],
        out_specs=row,
        out_shape=jax.ShapeDtypeStruct((T, D), F32),
        compiler_params=_cparams(("parallel",)),
        name="relu2_mlp",
    )(x2, mod_l, g_pre.reshape(1, D), g_post.reshape(1, D), w1, w2)


def _prep_w_in(w_in):
    col = jnp.ones((w_in.shape[1],), F32)
    col = col.at[RET_HEADS * RET_DK:2 * RET_HEADS * RET_DK].set(RET_DK ** -0.5)
    fq0 = G_FQ * LANE
    col = col.at[fq0:fq0 + FOX_HEADS * FOX_DH].set(FOX_DH ** -0.5)
    w = w_in * col[None, :]
    w_main = jnp.concatenate([w[:, :FF_COL], w[:, FF_COL + FOX_HEADS:]], axis=1).astype(BF16)
    w_ff = jnp.zeros((w_in.shape[0], LANE), F32).at[:, :FOX_HEADS].set(w[:, FF_COL:FF_COL + FOX_HEADS])
    return w_main, w_ff.astype(BF16)


def kernel(x, c, ada_w, ada_b, norm_pre_mix, norm_post_mix, norm_pre_mlp, norm_post_mlp, w_in, conv_w, conv_b,
           lru_w_a, lru_b_a, lru_w_x, lru_b_x, lru_lambda, fox_b_f, ret_w_o, lru_w_o, fox_w_o, w_out, mlp_w1,
           mlp_w2):
    B, S, D = x.shape
    L = ada_w.shape[0]
    assert D == D_MODEL and S % 1024 == 0
    ret_bt, lru_bt, fox_tq = 256, 256, 512
    mod = _modulation(c, ada_w, ada_b)
    tables = _retention_tables(S, ret_bt)
    x2 = x.reshape(B * S, D)
    for l in range(L):
        w_main, w_ff = _prep_w_in(w_in[l])
        P, ff = _in_projection(x2, mod[l], norm_pre_mix[l], w_main, w_ff, S)
        y_ret = _retention(P, tables, B, S, ret_bt)
        w_ax = jnp.concatenate([lru_w_a[l], lru_w_x[l]], axis=-1).astype(BF16)
        y_lru = _rg_lru(P, conv_w[l], conv_b[l], w_ax, lru_b_a[l], lru_b_x[l], lru_lambda[l], B, S, lru_bt)
        Aq, Ak = _forget_cumsum(ff, fox_b_f[l], B, S)
        y_fox = _fox_attention(P, Aq, Ak, B, S, fox_tq)
        x2 = _merge(y_ret, y_lru, y_fox, P, x2, mod[l], norm_post_mix[l], ret_w_o[l].astype(BF16),
                    lru_w_o[l].astype(BF16), fox_w_o[l].astype(BF16), w_out[l].astype(BF16), S)
        x2 = _mlp(x2, mod[l], norm_pre_mlp[l], norm_post_mlp[l], mlp_w1[l].astype(BF16),
                  mlp_w2[l].astype(BF16), S)
    return x2.reshape(B, S, D)
```

```python
import functools

import jax
import jax.numpy as jnp
from jax import lax
from jax.experimental import pallas as pl
from jax.experimental.pallas import tpu as pltpu

F32 = jnp.float32
BF16 = jnp.bfloat16

D_MODEL = 1024
EPS = 1e-6
ROPE_BASE = 10000.0
CHUNK = 64
RET_HEADS, RET_DK, RET_DV = 4, 128, 256
LRU_BLOCKS, LRU_BW, CONV_W, LRU_C = 8, 128, 4, 8.0
FOX_HEADS, FOX_DH = 8, 128
D_FF = 4 * D_MODEL
LANE = 128

G_RQ, G_RK, G_RV, G_RG, G_LX, G_LY, G_FQ, G_FK, G_FV, G_GATE = 0, 4, 8, 16, 24, 32, 40, 48, 56, 64
N_GROUPS = 88
FF_COL = 2 * RET_HEADS * RET_DK + 2 * RET_HEADS * RET_DV + 2 * D_MODEL + 3 * FOX_HEADS * FOX_DH

NEG = -1e30
LOG2E = 1.4426950408889634
VMEM_LIMIT = 56 * 1024 * 1024


def _cparams(sem):
    return pltpu.CompilerParams(dimension_semantics=sem, vmem_limit_bytes=VMEM_LIMIT)


def _sigmoid(z):
    return 1.0 / (1.0 + jnp.exp(-z))


def _softplus(z):
    return jnp.maximum(z, 0.0) + jnp.log1p(jnp.exp(-jnp.abs(z)))


def _rms(x, g):
    return x * lax.rsqrt(jnp.mean(x * x, axis=-1, keepdims=True) + EPS) * g


def _cat(ref, lo, n):
    return jnp.concatenate([ref[lo + k] for k in range(n)], axis=-1)


def _mod_kernel(c_ref, w_ref, b_ref, o_ref):
    c = c_ref[...]
    o_ref[0] = jnp.dot(c * _sigmoid(c), w_ref[0], preferred_element_type=F32,
                       precision=lax.Precision.HIGHEST) + b_ref[0]


def _modulation(c, ada_w, ada_b):
    L, D, D6 = ada_w.shape
    B = c.shape[0]
    c8 = jnp.zeros((8, D), F32).at[:B].set(c)
    tn = 1024
    out = pl.pallas_call(
        _mod_kernel,
        grid=(L, D6 // tn),
        in_specs=[
            pl.BlockSpec((8, D), lambda l, j: (0, 0)),
            pl.BlockSpec((1, D, tn), lambda l, j: (l, 0, j)),
            pl.BlockSpec((1, 1, tn), lambda l, j: (l, 0, j)),
        ],
        out_specs=pl.BlockSpec((1, 8, tn), lambda l, j: (l, 0, j)),
        out_shape=jax.ShapeDtypeStruct((L, 8, D6), F32),
        compiler_params=_cparams(("parallel", "parallel")),
        name="adaln_mod",
    )(c8, ada_w, ada_b.reshape(L, 1, D6))
    return out[:, :B].reshape(L, B, 1, D6)


def _inproj_kernel(x_ref, mod_ref, g_ref, w_ref, wff_ref, p_ref, ff_ref, h_scr):
    D = D_MODEL

    @pl.when(pl.program_id(1) == 0)
    def _():
        h = _rms(x_ref[...], g_ref[...]) * (1.0 + mod_ref[:, D:2 * D]) + mod_ref[:, 0:D]
        hb = h.astype(BF16)
        h_scr[...] = hb
        ff_ref[...] = jnp.dot(hb, wff_ref[...], preferred_element_type=F32)

    acc = jnp.dot(h_scr[...], w_ref[...], preferred_element_type=F32)
    for g in range(acc.shape[1] // LANE):
        p_ref[g] = acc[:, g * LANE:(g + 1) * LANE].astype(BF16)


def _in_projection(x2, mod_l, g_pre, w_main, w_ff, S, tm=1024, tn=1024):
    T, D = x2.shape
    n_main = w_main.shape[1]
    nt = S // tm
    return pl.pallas_call(
        _inproj_kernel,
        grid=(T // tm, n_main // tn),
        in_specs=[
            pl.BlockSpec((tm, D), lambda i, j: (i, 0)),
            pl.BlockSpec((None, 1, 6 * D), lambda i, j: (i // nt, 0, 0)),
            pl.BlockSpec((1, D), lambda i, j: (0, 0)),
            pl.BlockSpec((D, tn), lambda i, j: (0, j)),
            pl.BlockSpec((D, LANE), lambda i, j: (0, 0)),
        ],
        out_specs=[
            pl.BlockSpec((tn // LANE, tm, LANE), lambda i, j: (j, i, 0)),
            pl.BlockSpec((tm, LANE), lambda i, j: (i, 0)),
        ],
        out_shape=[
            jax.ShapeDtypeStruct((n_main // LANE, T, LANE), BF16),
            jax.ShapeDtypeStruct((T, LANE), F32),
        ],
        scratch_shapes=[pltpu.VMEM((tm, D), BF16)],
        compiler_params=_cparams(("parallel", "arbitrary")),
        name="in_proj",
    )(x2, mod_l, g_pre.reshape(1, D), w_main, w_ff)


def _ret_kernel(q_ref, k_ref, v_ref, rg_ref, cos_ref, sin_ref, dmat_ref, qdec_ref, kdec_ref, cdec_ref,
                o_ref, state):
    @pl.when(pl.program_id(1) == 0)
    def _():
        state[...] = jnp.zeros_like(state)

    cos = cos_ref[...]
    sin = sin_ref[...]
    half = RET_DK // 2
    for h in range(RET_HEADS):
        q = q_ref[h].astype(F32)
        k = k_ref[h].astype(F32)
        qr = q * cos + pltpu.roll(q, half, 1) * sin
        kr = k * cos + pltpu.roll(k, half, 1) * sin
        v = _cat(v_ref, 2 * h, 2)
        s = lax.dot_general(qr.astype(BF16), kr.astype(BF16), (((1,), (1,)), ((), ())),
                            preferred_element_type=F32) * dmat_ref[h]
        st = state[h]
        o = jnp.dot(s.astype(BF16), v, preferred_element_type=F32)
        o = o + jnp.dot((qr * qdec_ref[h]).astype(BF16), st.astype(BF16), preferred_element_type=F32)
        kd = (kr * kdec_ref[h]).astype(BF16)
        state[h] = st * cdec_ref[h] + lax.dot_general(kd, v, (((0,), (0,)), ((), ())),
                                                      preferred_element_type=F32)
        d = o - jnp.mean(o, axis=-1, keepdims=True)
        on = d * lax.rsqrt(jnp.mean(d * d, axis=-1, keepdims=True) + EPS)
        g = _cat(rg_ref, 2 * h, 2).astype(F32)
        o_ref[:, h * RET_DV:(h + 1) * RET_DV] = (g * _sigmoid(g) * on).astype(BF16)


def _retention_tables(S, bt):
    H = RET_HEADS
    log_g = jnp.log(1.0 - 2.0 ** (-5.0 - jnp.arange(H, dtype=F32)))
    idx = jnp.arange(bt, dtype=F32)
    dist = jnp.abs(idx[:, None] - idx[None, :])
    ch = jnp.arange(bt) // CHUNK
    allowed = ch[None, :] <= ch[:, None]
    dmat = jnp.where(allowed[None], jnp.exp(dist[None] * log_g[:, None, None]), 0.0)
    qdec = jnp.exp((idx + 1.0)[None, :] * log_g[:, None])
    kdec = jnp.exp((bt - 1.0 - idx)[None, :] * log_g[:, None])
    cdec = jnp.exp(bt * log_g)
    qdec = jnp.broadcast_to(qdec[:, :, None], (H, bt, LANE))
    kdec = jnp.broadcast_to(kdec[:, :, None], (H, bt, LANE))
    cdec = jnp.broadcast_to(cdec[:, None, None], (H, 1, RET_DV))
    half = RET_DK // 2
    inv = ROPE_BASE ** (-jnp.arange(half, dtype=F32) / half)
    ang = jnp.arange(S).astype(F32)[:, None] * inv[None, :]
    cos, sin = jnp.cos(ang), jnp.sin(ang)
    cos2 = jnp.concatenate([cos, cos], axis=-1)
    sin2 = jnp.concatenate([-sin, sin], axis=-1)
    return cos2, sin2, dmat, qdec, kdec, cdec


def _retention(P, tables, B, S, bt):
    T = B * S
    nt = S // bt
    cos2, sin2, dmat, qdec, kdec, cdec = tables
    H = RET_HEADS
    tok = lambda b, i: b * nt + i
    return pl.pallas_call(
        _ret_kernel,
        grid=(B, nt),
        in_specs=[
            pl.BlockSpec((4, bt, LANE), lambda b, i: (G_RQ // 4, tok(b, i), 0)),
            pl.BlockSpec((4, bt, LANE), lambda b, i: (G_RK // 4, tok(b, i), 0)),
            pl.BlockSpec((8, bt, LANE), lambda b, i: (G_RV // 8, tok(b, i), 0)),
            pl.BlockSpec((8, bt, LANE), lambda b, i: (G_RG // 8, tok(b, i), 0)),
            pl.BlockSpec((bt, LANE), lambda b, i: (i, 0)),
            pl.BlockSpec((bt, LANE), lambda b, i: (i, 0)),
            pl.BlockSpec((H, bt, bt), lambda b, i: (0, 0, 0)),
            pl.BlockSpec((H, bt, LANE), lambda b, i: (0, 0, 0)),
            pl.BlockSpec((H, bt, LANE), lambda b, i: (0, 0, 0)),
            pl.BlockSpec((H, 1, RET_DV), lambda b, i: (0, 0, 0)),
        ],
        out_specs=pl.BlockSpec((bt, H * RET_DV), lambda b, i: (tok(b, i), 0)),
        out_shape=jax.ShapeDtypeStruct((T, H * RET_DV), BF16),
        scratch_shapes=[pltpu.VMEM((H, RET_DK, RET_DV), F32)],
        compiler_params=_cparams(("parallel", "arbitrary")),
        name="retention",
    )(P, P, P, P, cos2, sin2, dmat, qdec, kdec, cdec)


def _lru_kernel(lx_ref, ly_ref, cw_ref, cb_ref, wax_ref, ba_ref, bx_ref, lam_ref, o_ref,
                tail, hprev, a_scr, b_scr, h_scr):
    bt = lx_ref.shape[1]

    @pl.when(pl.program_id(1) == 0)
    def _():
        tail[...] = jnp.zeros_like(tail)
        hprev[...] = jnp.zeros_like(hprev)

    r8 = lax.broadcasted_iota(jnp.int32, (bt, LANE), 0) & 7
    for n in range(LRU_BLOCKS):
        sl = slice(n * LANE, (n + 1) * LANE)
        x = lx_ref[n].astype(F32)
        xp = jnp.concatenate([tail[:, sl], x], axis=0)
        u = cb_ref[:, sl] + xp[8:8 + bt] * cw_ref[3:4, sl]
        for i in range(CONV_W - 1):
            u = u + xp[5 + i:5 + i + bt] * cw_ref[i:i + 1, sl]
        tail[:, sl] = x[bt - 8:bt]
        gates = jnp.dot(u.astype(BF16), wax_ref[n], preferred_element_type=F32)
        r = _sigmoid(gates[:, :LANE] + ba_ref[:, sl])
        ig = _sigmoid(gates[:, LANE:] + bx_ref[:, sl])
        log_a = (-LRU_C) * r * _softplus(-lam_ref[:, sl])
        a = jnp.exp(log_a)
        bb = jnp.sqrt(jnp.maximum(1.0 - a * a, 0.0)) * (ig * u)
        for d in (1, 2, 4):
            m = r8 >= d
            a_sh = jnp.where(m, pltpu.roll(a, d, 0), 1.0)
            b_sh = jnp.where(m, pltpu.roll(bb, d, 0), 0.0)
            bb = a * b_sh + bb
            a = a * a_sh
        a_scr[:, sl] = a
        b_scr[:, sl] = bb

    def body(c, hp):
        r0 = pl.multiple_of(c * 8, 8)
        h = a_scr[pl.ds(r0, 8), :] * hp + b_scr[pl.ds(r0, 8), :]
        h_scr[pl.ds(r0, 8), :] = h
        return jnp.broadcast_to(h[7:8, :], h.shape)

    hprev[...] = lax.fori_loop(0, bt // 8, body, hprev[...], unroll=4)
    for n in range(LRU_BLOCKS):
        sl = slice(n * LANE, (n + 1) * LANE)
        o_ref[:, sl] = (jax.nn.gelu(ly_ref[n].astype(F32)) * h_scr[:, sl]).astype(BF16)


def _rg_lru(P, conv_w, conv_b, w_ax, b_a, b_x, lam, B, S, bt):
    T = B * S
    nt = S // bt
    W = LRU_BLOCKS * LRU_BW
    tok = lambda b, i: b * nt + i
    full = lambda shape: pl.BlockSpec(shape, lambda b, i: (0,) * len(shape))
    return pl.pallas_call(
        _lru_kernel,
        grid=(B, nt),
        in_specs=[
            pl.BlockSpec((8, bt, LANE), lambda b, i: (G_LX // 8, tok(b, i), 0)),
            pl.BlockSpec((8, bt, LANE), lambda b, i: (G_LY // 8, tok(b, i), 0)),
            full((CONV_W, W)), full((1, W)), full((LRU_BLOCKS, LRU_BW, 2 * LRU_BW)),
            full((1, W)), full((1, W)), full((1, W)),
        ],
        out_specs=pl.BlockSpec((bt, W), lambda b, i: (tok(b, i), 0)),
        out_shape=jax.ShapeDtypeStruct((T, W), BF16),
        scratch_shapes=[pltpu.VMEM((8, W), F32), pltpu.VMEM((8, W), F32),
                        pltpu.VMEM((bt, W), F32), pltpu.VMEM((bt, W), F32), pltpu.VMEM((bt, W), F32)],
        compiler_params=_cparams(("parallel", "arbitrary")),
        name="rg_lru",
    )(P, P, conv_w, conv_b.reshape(1, W), w_ax, b_a.reshape(1, W), b_x.reshape(1, W), lam.reshape(1, W))


def _cum_kernel(ff_ref, bf_ref, tri_ref, aq_ref, ak_ref, cl_ref, carry):
    bc = ff_ref.shape[0]

    @pl.when(pl.program_id(1) == 0)
    def _():
        carry[...] = jnp.zeros_like(carry)

    z = ff_ref[...] + bf_ref[...]
    logf = jnp.minimum(z, 0.0) - jnp.log1p(jnp.exp(-jnp.abs(z)))
    cum = jnp.dot(tri_ref[...], logf, preferred_element_type=F32, precision=lax.Precision.HIGHEST) + carry[...]
    carry[...] = cum[bc - 1:bc]
    cum = cum * LOG2E
    cl_ref[0] = cum[bc - 1:bc]
    lane = lax.broadcasted_iota(jnp.int32, (bc, LANE), 1)
    for h in range(FOX_HEADS):
        c = jnp.broadcast_to(cum[:, h:h + 1], (bc, LANE))
        hi = c.astype(BF16).astype(F32)
        r1 = c - hi
        mid = r1.astype(BF16).astype(F32)
        lo = r1 - mid
        aq = jnp.where(lane == 0, hi, jnp.where(lane == 1, mid, jnp.where(lane == 2, lo,
                       jnp.where(lane < 6, 1.0, 0.0))))
        ak = jnp.where(lane < 3, 1.0, jnp.where(lane == 3, -hi, jnp.where(lane == 4, -mid,
                       jnp.where(lane == 5, -lo, 0.0))))
        aq_ref[h] = aq.astype(BF16)
        ak_ref[h] = ak.astype(BF16)


def _forget_cumsum(ff, b_f, B, S, tq, bc=256):
    T = B * S
    nt = S // bc
    bf = jnp.zeros((1, LANE), F32).at[0, :FOX_HEADS].set(b_f)
    tri = (jnp.arange(bc)[None, :] <= jnp.arange(bc)[:, None]).astype(F32)
    tok = lambda b, i: b * nt + i
    out = jax.ShapeDtypeStruct((FOX_HEADS, T, LANE), BF16)
    aq, ak, cl = pl.pallas_call(
        _cum_kernel,
        grid=(B, nt),
        in_specs=[
            pl.BlockSpec((bc, LANE), lambda b, i: (tok(b, i), 0)),
            pl.BlockSpec((1, LANE), lambda b, i: (0, 0)),
            pl.BlockSpec((bc, bc), lambda b, i: (0, 0)),
        ],
        out_specs=[pl.BlockSpec((FOX_HEADS, bc, LANE), lambda b, i: (0, tok(b, i), 0))] * 2
        + [pl.BlockSpec((1, 1, LANE), lambda b, i: (tok(b, i), 0, 0))],
        out_shape=[out, out, jax.ShapeDtypeStruct((B * nt, 1, LANE), F32)],
        scratch_shapes=[pltpu.VMEM((1, LANE), F32)],
        compiler_params=_cparams(("parallel", "arbitrary")),
        name="forget_cumsum",
    )(ff, bf, tri)
    per = tq // bc
    nq = S // tq
    last = cl.reshape(B, nt, LANE)[:, per - 1::per, :FOX_HEADS]
    ncl = jnp.zeros((B, FOX_HEADS, LANE), F32).at[:, :, :nq].set(-last.transpose(0, 2, 1))
    return aq, ak, ncl.reshape(B * FOX_HEADS, 1, LANE)


FOX_UNROLL = 4
SUB = 256
SKIP_MARGIN = 130.0


def _fox_kernel(ncl_ref, q_ref, aq_ref, k_ref, ak_ref, v_ref, o_ref, m_scr, acc_scr, kmax_scr):
    tq = q_ref.shape[1]
    S = k_ref.shape[1]
    i = pl.program_id(1)

    @pl.when(i == 0)
    def _():
        def body(c, mx):
            r0 = pl.multiple_of(c * tq, tq)
            kc = k_ref[0, pl.ds(r0, tq), :].astype(F32)
            n2 = jnp.sum(kc * kc, axis=1, keepdims=True)
            return jnp.maximum(mx, jnp.max(n2, axis=0, keepdims=True))

        mx = lax.fori_loop(0, S // tq, body, jnp.zeros((1, 1), F32))
        kmax_scr[...] = jnp.broadcast_to(jnp.sqrt(mx), kmax_scr.shape)

    q = q_ref[0]
    aq = aq_ref[0]
    qa = jnp.concatenate([q, aq], axis=-1)
    ones = jnp.ones((SUB, LANE), BF16)
    nsub = tq // SUB

    def scores(j):
        out = []
        for h in range(nsub):
            r0 = pl.multiple_of(j * tq + h * SUB, SUB)
            ka = jnp.concatenate([k_ref[0, pl.ds(r0, SUB), :], ak_ref[0, pl.ds(r0, SUB), :]], axis=-1)
            out.append(lax.dot_general(qa, ka, (((1,), (1,)), ((), ())), preferred_element_type=F32))
        return out

    def update(s, j, m_prev, acc):
        mx = jnp.max(s[0], axis=1, keepdims=True)
        for sh in s[1:]:
            mx = jnp.maximum(mx, jnp.max(sh, axis=1, keepdims=True))
        m_new = jnp.broadcast_to(mx, m_scr.shape) if m_prev is None else jnp.maximum(m_prev, mx)
        m_rep = jnp.concatenate([m_new] * (SUB // LANE), axis=1)
        pv = None
        for h, sh in enumerate(s):
            r0 = pl.multiple_of(j * tq + h * SUB, SUB)
            va = jnp.concatenate([v_ref[0, pl.ds(r0, SUB), :], ones], axis=-1)
            d = jnp.dot(jnp.exp2(sh - m_rep).astype(BF16), va, preferred_element_type=F32)
            pv = d if pv is None else pv + d
        if m_prev is not None:
            alpha = jnp.exp2(m_prev - m_new)
            pv = acc * jnp.concatenate([alpha, alpha], axis=-1) + pv
        return m_new, pv

    def run(js):
        s_list = [scores(j) for j in js]
        m, acc = m_scr[...], acc_scr[...]
        for s, j in zip(s_list, js):
            m, acc = update(s, j, m, acc)
        m_scr[...] = m
        acc_scr[...] = acc

    row = lax.broadcasted_iota(jnp.int32, (tq, SUB), 0)
    col = lax.broadcasted_iota(jnp.int32, (tq, SUB), 1)
    s = [jnp.where(col + h * SUB <= row, sh, NEG) for h, sh in enumerate(scores(i))]
    m_scr[...], acc_scr[...] = update(s, i, None, None)

    qf = q.astype(F32)
    qn = jnp.sqrt(jnp.sum(qf * qf, axis=1, keepdims=True))
    aqf = aq.astype(F32)
    c_t = aqf[:, 0:1] + aqf[:, 1:2] + aqf[:, 2:3]
    r = m_scr[:, 0:1] - SKIP_MARGIN - c_t - qn * kmax_scr[0:1, 0:1]
    r_min = jnp.min(r, axis=0, keepdims=True)
    lane = lax.broadcasted_iota(jnp.int32, (1, LANE), 1)
    keep = jnp.logical_and(lane < i, ncl_ref[...] >= r_min)
    n = jnp.sum(jnp.where(keep, 1.0, 0.0)).astype(jnp.int32)

    def group(t, carry):
        j0 = i - 1 - t * FOX_UNROLL
        run([j0 - u for u in range(FOX_UNROLL)])
        return carry

    ng = n // FOX_UNROLL
    lax.fori_loop(0, ng, group, 0)
    done = ng * FOX_UNROLL
    size = FOX_UNROLL // 2
    while size >= 1:
        take = ((n - done) & size) != 0
        j0 = i - 1 - done

        @pl.when(take)
        def _(j0=j0, size=size):
            run([j0 - u for u in range(size)])

        done = done + jnp.where(take, size, 0)
        size //= 2
    acc = acc_scr[...]
    o_ref[...] = (acc[:, :LANE] / acc[:, LANE:]).astype(BF16)


def _fox_attention(P, Aq, Ak, ncl, B, S, tq):
    T = B * S
    nq = S // tq
    H = FOX_HEADS
    assert nq <= LANE
    return pl.pallas_call(
        _fox_kernel,
        grid=(B * H, nq),
        in_specs=[
            pl.BlockSpec((None, 1, LANE), lambda g, i: (g, 0, 0)),
            pl.BlockSpec((1, tq, LANE), lambda g, i: (G_FQ + g % H, (g // H) * nq + i, 0)),
            pl.BlockSpec((1, tq, LANE), lambda g, i: (g % H, (g // H) * nq + i, 0)),
            pl.BlockSpec((1, S, LANE), lambda g, i: (G_FK + g % H, g // H, 0)),
            pl.BlockSpec((1, S, LANE), lambda g, i: (g % H, g // H, 0)),
            pl.BlockSpec((1, S, LANE), lambda g, i: (G_FV + g % H, g // H, 0)),
        ],
        out_specs=pl.BlockSpec((tq, LANE), lambda g, i: ((g // H) * nq + i, g % H)),
        out_shape=jax.ShapeDtypeStruct((T, H * FOX_DH), BF16),
        scratch_shapes=[pltpu.VMEM((tq, LANE), F32), pltpu.VMEM((tq, 2 * LANE), F32),
                        pltpu.VMEM((8, LANE), F32)],
        compiler_params=_cparams(("parallel", "arbitrary")),
        name="fox_attention",
    )(ncl, P, Aq, P, Ak, P)


def _merge_kernel(yr_ref, yl_ref, yf_ref, g0_ref, g1_ref, g2_ref, x_ref, mod_ref, gp_ref,
                  wr_ref, wl_ref, wf_ref, wo_ref, o_ref):
    D = D_MODEL

    def branch(y_ref, w_ref, g_ref):
        g = _sigmoid(_cat(g_ref, 0, 8).astype(F32))
        return g * jnp.dot(y_ref[...], w_ref[...], preferred_element_type=F32)

    m = branch(yr_ref, wr_ref, g0_ref) + branch(yl_ref, wl_ref, g1_ref) + branch(yf_ref, wf_ref, g2_ref)
    z = jnp.dot(m.astype(BF16), wo_ref[...], preferred_element_type=F32)
    o_ref[...] = x_ref[...] + mod_ref[:, 2 * D:3 * D] * _rms(z, gp_ref[...])


def _merge(y_ret, y_lru, y_fox, P, x2, mod_l, g_post, w_r, w_l, w_f, w_o, S, tm=512):
    T, D = x2.shape
    nt = S // tm
    row = pl.BlockSpec((tm, D), lambda i: (i, 0))
    wspec = pl.BlockSpec((D, D), lambda i: (0, 0))
    gate = lambda k: pl.BlockSpec((8, tm, LANE), lambda i: (G_GATE // 8 + k, i, 0))
    return pl.pallas_call(
        _merge_kernel,
        grid=(T // tm,),
        in_specs=[row, row, row, gate(0), gate(1), gate(2), row,
                  pl.BlockSpec((None, 1, 6 * D), lambda i: (i // nt, 0, 0)),
                  pl.BlockSpec((1, D), lambda i: (0, 0)),
                  wspec, wspec, wspec, wspec],
        out_specs=row,
        out_shape=jax.ShapeDtypeStruct((T, D), F32),
        compiler_params=_cparams(("parallel",)),
        name="merge_out_proj",
    )(y_ret, y_lru, y_fox, P, P, P, x2, mod_l, g_post.reshape(1, D), w_r, w_l, w_f, w_o)


def _mlp_kernel(x_ref, mod_ref, gpre_ref, gpost_ref, w1_ref, w2_ref, o_ref):
    D = D_MODEL
    x = x_ref[...]
    h = (_rms(x, gpre_ref[...]) * (1.0 + mod_ref[:, 4 * D:5 * D]) + mod_ref[:, 3 * D:4 * D]).astype(BF16)
    f = jnp.zeros(x.shape, F32)
    ck = 1024
    for c in range(D_FF // ck):
        a = jnp.maximum(jnp.dot(h, w1_ref[:, c * ck:(c + 1) * ck], preferred_element_type=F32), 0.0)
        f = f + jnp.dot((a * a).astype(BF16), w2_ref[c * ck:(c + 1) * ck, :], preferred_element_type=F32)
    o_ref[...] = x + mod_ref[:, 5 * D:6 * D] * _rms(f, gpost_ref[...])


def _mlp(x2, mod_l, g_pre, g_post, w1, w2, S, tm=512):
    T, D = x2.shape
    nt = S // tm
    row = pl.BlockSpec((tm, D), lambda i: (i, 0))
    vec = pl.BlockSpec((1, D), lambda i: (0, 0))
    return pl.pallas_call(
        _mlp_kernel,
        grid=(T // tm,),
        in_specs=[row, pl.BlockSpec((None, 1, 6 * D), lambda i: (i // nt, 0, 0)), vec, vec,
                  pl.BlockSpec((D, D_FF), lambda i: (0, 0)), pl.BlockSpec((D_FF, D), lambda i: (0, 0))],
        out_specs=row,
        out_shape=jax.ShapeDtypeStruct((T, D), F32),
        compiler_params=_cparams(("parallel",)),
        name="relu2_mlp",
    )(x2, mod_l, g_pre.reshape(1, D), g_post.reshape(1, D), w1, w2)


def _prep_w_in(w_in):
    col = jnp.ones((w_in.shape[1],), F32)
    col = col.at[RET_HEADS * RET_DK:2 * RET_HEADS * RET_DK].set(RET_DK ** -0.5)
    fq0 = G_FQ * LANE
    col = col.at[fq0:fq0 + FOX_HEADS * FOX_DH].set(FOX_DH ** -0.5 * LOG2E)
    w = w_in * col[None, :]
    w_main = jnp.concatenate([w[:, :FF_COL], w[:, FF_COL + FOX_HEADS:]], axis=1).astype(BF16)
    w_ff = jnp.zeros((w_in.shape[0], LANE), F32).at[:, :FOX_HEADS].set(w[:, FF_COL:FF_COL + FOX_HEADS])
    return w_main, w_ff.astype(BF16)


def kernel(x, c, ada_w, ada_b, norm_pre_mix, norm_post_mix, norm_pre_mlp, norm_post_mlp, w_in, conv_w, conv_b,
           lru_w_a, lru_b_a, lru_w_x, lru_b_x, lru_lambda, fox_b_f, ret_w_o, lru_w_o, fox_w_o, w_out, mlp_w1,
           mlp_w2):
    B, S, D = x.shape
    L = ada_w.shape[0]
    assert D == D_MODEL and S % 1024 == 0
    ret_bt, lru_bt, fox_tq = 256, 256, 512
    mod = _modulation(c, ada_w, ada_b)
    tables = _retention_tables(S, ret_bt)
    x2 = x.reshape(B * S, D)
    for l in range(L):
        w_main, w_ff = _prep_w_in(w_in[l])
        P, ff = _in_projection(x2, mod[l], norm_pre_mix[l], w_main, w_ff, S)
        y_ret = _retention(P, tables, B, S, ret_bt)
        w_ax = jnp.concatenate([lru_w_a[l], lru_w_x[l]], axis=-1).astype(BF16)
        y_lru = _rg_lru(P, conv_w[l], conv_b[l], w_ax, lru_b_a[l], lru_b_x[l], lru_lambda[l], B, S, lru_bt)
        Aq, Ak, ncl = _forget_cumsum(ff, fox_b_f[l], B, S, fox_tq)
        y_fox = _fox_attention(P, Aq, Ak, ncl, B, S, fox_tq)
        x2 = _merge(y_ret, y_lru, y_fox, P, x2, mod[l], norm_post_mix[l], ret_w_o[l].astype(BF16),
                    lru_w_o[l].astype(BF16), fox_w_o[l].astype(BF16), w_out[l].astype(BF16), S)
        x2 = _mlp(x2, mod[l], norm_pre_mlp[l], norm_post_mlp[l], mlp_w1[l].astype(BF16),
                  mlp_w2[l].astype(BF16), S)
    return x2.reshape(B, S, D)
```

```python
import functools

import jax
import jax.numpy as jnp
from jax import lax
from jax.experimental import pallas as pl
from jax.experimental.pallas import tpu as pltpu

F32 = jnp.float32
BF16 = jnp.bfloat16

D_MODEL = 1024
EPS = 1e-6
ROPE_BASE = 10000.0
CHUNK = 64
RET_HEADS, RET_DK, RET_DV = 4, 128, 256
LRU_BLOCKS, LRU_BW, CONV_W, LRU_C = 8, 128, 4, 8.0
FOX_HEADS, FOX_DH = 8, 128
D_FF = 4 * D_MODEL
LANE = 128

G_RQ, G_RK, G_RV, G_RG, G_LX, G_LY, G_FQ, G_FK, G_FV, G_GATE = 0, 4, 8, 16, 24, 32, 40, 48, 56, 64
N_GROUPS = 88
FF_COL = 2 * RET_HEADS * RET_DK + 2 * RET_HEADS * RET_DV + 2 * D_MODEL + 3 * FOX_HEADS * FOX_DH

NEG = -1e30
F32_TINY = 1.1754944e-38
LOG2E = 1.4426950408889634
VMEM_LIMIT = 56 * 1024 * 1024


def _cparams(sem):
    return pltpu.CompilerParams(dimension_semantics=sem, vmem_limit_bytes=VMEM_LIMIT)


def _sigmoid(z):
    return 1.0 / (1.0 + jnp.exp(-z))


def _softplus(z):
    return jnp.maximum(z, 0.0) + jnp.log1p(jnp.exp(-jnp.abs(z)))


def _rms(x, g):
    return x * lax.rsqrt(jnp.mean(x * x, axis=-1, keepdims=True) + EPS) * g


def _cat(ref, lo, n):
    return jnp.concatenate([ref[lo + k] for k in range(n)], axis=-1)


def _mod_kernel(c_ref, w_ref, b_ref, o_ref):
    c = c_ref[...]
    o_ref[0] = jnp.dot(c * _sigmoid(c), w_ref[0], preferred_element_type=F32,
                       precision=lax.Precision.HIGHEST) + b_ref[0]


def _modulation(c, ada_w, ada_b):
    L, D, D6 = ada_w.shape
    B = c.shape[0]
    c8 = jnp.zeros((8, D), F32).at[:B].set(c)
    tn = 1024
    out = pl.pallas_call(
        _mod_kernel,
        grid=(L, D6 // tn),
        in_specs=[
            pl.BlockSpec((8, D), lambda l, j: (0, 0)),
            pl.BlockSpec((1, D, tn), lambda l, j: (l, 0, j)),
            pl.BlockSpec((1, 1, tn), lambda l, j: (l, 0, j)),
        ],
        out_specs=pl.BlockSpec((1, 8, tn), lambda l, j: (l, 0, j)),
        out_shape=jax.ShapeDtypeStruct((L, 8, D6), F32),
        compiler_params=_cparams(("parallel", "parallel")),
        name="adaln_mod",
    )(c8, ada_w, ada_b.reshape(L, 1, D6))
    return out[:, :B].reshape(L, B, 1, D6)


def _inproj_kernel(x_ref, mod_ref, g_ref, w_ref, wff_ref, p_ref, ff_ref, h_scr):
    D = D_MODEL

    @pl.when(pl.program_id(1) == 0)
    def _():
        h = _rms(x_ref[...], g_ref[...]) * (1.0 + mod_ref[:, D:2 * D]) + mod_ref[:, 0:D]
        hb = h.astype(BF16)
        h_scr[...] = hb
        ff_ref[...] = jnp.dot(hb, wff_ref[...], preferred_element_type=F32)

    acc = jnp.dot(h_scr[...], w_ref[...], preferred_element_type=F32)
    for g in range(acc.shape[1] // LANE):
        p_ref[g] = acc[:, g * LANE:(g + 1) * LANE].astype(BF16)


def _in_projection(x2, mod_l, g_pre, w_main, w_ff, S, tm=1024, tn=1024):
    T, D = x2.shape
    n_main = w_main.shape[1]
    nt = S // tm
    return pl.pallas_call(
        _inproj_kernel,
        grid=(T // tm, n_main // tn),
        in_specs=[
            pl.BlockSpec((tm, D), lambda i, j: (i, 0)),
            pl.BlockSpec((None, 1, 6 * D), lambda i, j: (i // nt, 0, 0)),
            pl.BlockSpec((1, D), lambda i, j: (0, 0)),
            pl.BlockSpec((D, tn), lambda i, j: (0, j)),
            pl.BlockSpec((D, LANE), lambda i, j: (0, 0)),
        ],
        out_specs=[
            pl.BlockSpec((tn // LANE, tm, LANE), lambda i, j: (j, i, 0)),
            pl.BlockSpec((tm, LANE), lambda i, j: (i, 0)),
        ],
        out_shape=[
            jax.ShapeDtypeStruct((n_main // LANE, T, LANE), BF16),
            jax.ShapeDtypeStruct((T, LANE), F32),
        ],
        scratch_shapes=[pltpu.VMEM((tm, D), BF16)],
        compiler_params=_cparams(("parallel", "arbitrary")),
        name="in_proj",
    )(x2, mod_l, g_pre.reshape(1, D), w_main, w_ff)


def _ret_kernel(q_ref, k_ref, v_ref, rg_ref, cos_ref, sin_ref, dmat_ref, qdec_ref, kdec_ref, cdec_ref,
                o_ref, state):
    @pl.when(pl.program_id(1) == 0)
    def _():
        state[...] = jnp.zeros_like(state)

    cos = cos_ref[...]
    sin = sin_ref[...]
    half = RET_DK // 2
    for h in range(RET_HEADS):
        q = q_ref[h].astype(F32)
        k = k_ref[h].astype(F32)
        qr = q * cos + pltpu.roll(q, half, 1) * sin
        kr = k * cos + pltpu.roll(k, half, 1) * sin
        v = _cat(v_ref, 2 * h, 2)
        s = lax.dot_general(qr.astype(BF16), kr.astype(BF16), (((1,), (1,)), ((), ())),
                            preferred_element_type=F32) * dmat_ref[h]
        st = state[h]
        o = jnp.dot(s.astype(BF16), v, preferred_element_type=F32)
        o = o + jnp.dot((qr * qdec_ref[h]).astype(BF16), st.astype(BF16), preferred_element_type=F32)
        kd = (kr * kdec_ref[h]).astype(BF16)
        state[h] = st * cdec_ref[h] + lax.dot_general(kd, v, (((0,), (0,)), ((), ())),
                                                      preferred_element_type=F32)
        d = o - jnp.mean(o, axis=-1, keepdims=True)
        on = d * lax.rsqrt(jnp.mean(d * d, axis=-1, keepdims=True) + EPS)
        g = _cat(rg_ref, 2 * h, 2).astype(F32)
        o_ref[:, h * RET_DV:(h + 1) * RET_DV] = (g * _sigmoid(g) * on).astype(BF16)


def _retention_tables(S, bt):
    H = RET_HEADS
    log_g = jnp.log(1.0 - 2.0 ** (-5.0 - jnp.arange(H, dtype=F32)))
    idx = jnp.arange(bt, dtype=F32)
    dist = jnp.abs(idx[:, None] - idx[None, :])
    ch = jnp.arange(bt) // CHUNK
    allowed = ch[None, :] <= ch[:, None]
    dmat = jnp.where(allowed[None], jnp.exp(dist[None] * log_g[:, None, None]), 0.0)
    qdec = jnp.exp((idx + 1.0)[None, :] * log_g[:, None])
    kdec = jnp.exp((bt - 1.0 - idx)[None, :] * log_g[:, None])
    cdec = jnp.exp(bt * log_g)
    qdec = jnp.broadcast_to(qdec[:, :, None], (H, bt, LANE))
    kdec = jnp.broadcast_to(kdec[:, :, None], (H, bt, LANE))
    cdec = jnp.broadcast_to(cdec[:, None, None], (H, 1, RET_DV))
    half = RET_DK // 2
    inv = ROPE_BASE ** (-jnp.arange(half, dtype=F32) / half)
    ang = jnp.arange(S).astype(F32)[:, None] * inv[None, :]
    cos, sin = jnp.cos(ang), jnp.sin(ang)
    cos2 = jnp.concatenate([cos, cos], axis=-1)
    sin2 = jnp.concatenate([-sin, sin], axis=-1)
    return cos2, sin2, dmat, qdec, kdec, cdec


def _retention(P, tables, B, S, bt):
    T = B * S
    nt = S // bt
    cos2, sin2, dmat, qdec, kdec, cdec = tables
    H = RET_HEADS
    tok = lambda b, i: b * nt + i
    return pl.pallas_call(
        _ret_kernel,
        grid=(B, nt),
        in_specs=[
            pl.BlockSpec((4, bt, LANE), lambda b, i: (G_RQ // 4, tok(b, i), 0)),
            pl.BlockSpec((4, bt, LANE), lambda b, i: (G_RK // 4, tok(b, i), 0)),
            pl.BlockSpec((8, bt, LANE), lambda b, i: (G_RV // 8, tok(b, i), 0)),
            pl.BlockSpec((8, bt, LANE), lambda b, i: (G_RG // 8, tok(b, i), 0)),
            pl.BlockSpec((bt, LANE), lambda b, i: (i, 0)),
            pl.BlockSpec((bt, LANE), lambda b, i: (i, 0)),
            pl.BlockSpec((H, bt, bt), lambda b, i: (0, 0, 0)),
            pl.BlockSpec((H, bt, LANE), lambda b, i: (0, 0, 0)),
            pl.BlockSpec((H, bt, LANE), lambda b, i: (0, 0, 0)),
            pl.BlockSpec((H, 1, RET_DV), lambda b, i: (0, 0, 0)),
        ],
        out_specs=pl.BlockSpec((bt, H * RET_DV), lambda b, i: (tok(b, i), 0)),
        out_shape=jax.ShapeDtypeStruct((T, H * RET_DV), BF16),
        scratch_shapes=[pltpu.VMEM((H, RET_DK, RET_DV), F32)],
        compiler_params=_cparams(("parallel", "arbitrary")),
        name="retention",
    )(P, P, P, P, cos2, sin2, dmat, qdec, kdec, cdec)


def _lru_kernel(lx_ref, ly_ref, cw_ref, cb_ref, wax_ref, ba_ref, bx_ref, lam_ref, o_ref,
                tail, hprev, a_scr, b_scr, h_scr):
    bt = lx_ref.shape[1]

    @pl.when(pl.program_id(1) == 0)
    def _():
        tail[...] = jnp.zeros_like(tail)
        hprev[...] = jnp.zeros_like(hprev)

    r8 = lax.broadcasted_iota(jnp.int32, (bt // 8, 8, LANE), 1)
    for n in range(LRU_BLOCKS):
        sl = slice(n * LANE, (n + 1) * LANE)
        x = lx_ref[n].astype(F32)
        xp = jnp.concatenate([tail[:, sl], x], axis=0)
        u = cb_ref[:, sl] + x * cw_ref[CONV_W - 1:CONV_W, sl]
        for k in range(1, CONV_W):
            u = u + pltpu.roll(xp, k, 0)[8:] * cw_ref[CONV_W - 1 - k:CONV_W - k, sl]
        tail[:, sl] = x[bt - 8:bt]
        gates = jnp.dot(u.astype(BF16), wax_ref[n], preferred_element_type=F32)
        r = 1.0 / (1.0 + jnp.exp2(gates[:, :LANE] + ba_ref[:, sl]))
        ig = 1.0 / (1.0 + jnp.exp2(gates[:, LANE:] + bx_ref[:, sl]))
        a = jnp.exp2(r * ((-LRU_C * LOG2E) * _softplus(-lam_ref[:, sl])))
        v = jnp.maximum(1.0 - a * a, 0.0)
        bb = v * lax.rsqrt(jnp.maximum(v, F32_TINY)) * (ig * u)
        a = a.reshape(bt // 8, 8, LANE)
        bb = bb.reshape(bt // 8, 8, LANE)
        for d in (1, 2, 4):
            m = r8 >= d
            a_sh = jnp.where(m, pltpu.roll(a, d, 1), 1.0)
            b_sh = jnp.where(m, pltpu.roll(bb, d, 1), 0.0)
            bb = a * b_sh + bb
            a = a * a_sh
        a_scr[:, sl] = a.reshape(bt, LANE)
        b_scr[:, sl] = bb.reshape(bt, LANE)

    def body(c, hp):
        r0 = pl.multiple_of(c * 8, 8)
        h = a_scr[pl.ds(r0, 8), :] * hp + b_scr[pl.ds(r0, 8), :]
        h_scr[pl.ds(r0, 8), :] = h
        return jnp.broadcast_to(h[7:8, :], h.shape)

    hprev[...] = lax.fori_loop(0, bt // 8, body, hprev[...], unroll=4)
    for n in range(LRU_BLOCKS):
        sl = slice(n * LANE, (n + 1) * LANE)
        o_ref[:, sl] = (jax.nn.gelu(ly_ref[n].astype(F32)) * h_scr[:, sl]).astype(BF16)


def _rg_lru(P, conv_w, conv_b, w_ax, b_a, b_x, lam, B, S, bt):
    T = B * S
    nt = S // bt
    W = LRU_BLOCKS * LRU_BW
    tok = lambda b, i: b * nt + i
    full = lambda shape: pl.BlockSpec(shape, lambda b, i: (0,) * len(shape))
    return pl.pallas_call(
        _lru_kernel,
        grid=(B, nt),
        in_specs=[
            pl.BlockSpec((8, bt, LANE), lambda b, i: (G_LX // 8, tok(b, i), 0)),
            pl.BlockSpec((8, bt, LANE), lambda b, i: (G_LY // 8, tok(b, i), 0)),
            full((CONV_W, W)), full((1, W)), full((LRU_BLOCKS, LRU_BW, 2 * LRU_BW)),
            full((1, W)), full((1, W)), full((1, W)),
        ],
        out_specs=pl.BlockSpec((bt, W), lambda b, i: (tok(b, i), 0)),
        out_shape=jax.ShapeDtypeStruct((T, W), BF16),
        scratch_shapes=[pltpu.VMEM((8, W), F32), pltpu.VMEM((8, W), F32),
                        pltpu.VMEM((bt, W), F32), pltpu.VMEM((bt, W), F32), pltpu.VMEM((bt, W), F32)],
        compiler_params=_cparams(("parallel", "arbitrary")),
        name="rg_lru",
    )(P, P, conv_w, conv_b.reshape(1, W), w_ax, b_a.reshape(1, W), b_x.reshape(1, W), lam.reshape(1, W))


def _cum_kernel(ff_ref, bf_ref, tri_ref, route_ref, const_ref, aq_ref, ak_ref, cl_ref, carry):
    bc = ff_ref.shape[0]

    @pl.when(pl.program_id(1) == 0)
    def _():
        carry[...] = jnp.zeros_like(carry)

    z = ff_ref[...] + bf_ref[...]
    logf = jnp.minimum(z, 0.0) - jnp.log1p(jnp.exp(-jnp.abs(z)))
    l1 = logf.astype(BF16)
    e1 = logf - l1.astype(F32)
    l2 = e1.astype(BF16)
    l3 = (e1 - l2.astype(F32)).astype(BF16)
    cs = jnp.dot(tri_ref[...], jnp.concatenate([l1, l2, l3], axis=1), preferred_element_type=F32)
    cum = (cs[:, :LANE] + cs[:, LANE:2 * LANE]) + cs[:, 2 * LANE:] + carry[...]
    carry[...] = cum[bc - 1:bc]
    cum = cum * LOG2E
    cl_ref[0] = cum[bc - 1:bc]
    hi = cum.astype(BF16).astype(F32)
    r1 = cum - hi
    mid = r1.astype(BF16).astype(F32)
    lo = r1 - mid
    lane = lax.broadcasted_iota(jnp.int32, (bc, LANE), 1)
    H = FOX_HEADS
    pieces = jnp.where(lane < H, hi, jnp.where(lane < 2 * H, pltpu.roll(mid, H, 1), pltpu.roll(lo, 2 * H, 1)))
    out = jnp.dot(pieces.astype(BF16), route_ref[...], preferred_element_type=F32) + const_ref[...]
    for h in range(H):
        aq_ref[h] = out[:, h * LANE:(h + 1) * LANE].astype(BF16)
        ak_ref[h] = out[:, (H + h) * LANE:(H + h + 1) * LANE].astype(BF16)


def _forget_cumsum(ff, b_f, B, S, tq, bc=256):
    T = B * S
    nt = S // bc
    bf = jnp.zeros((1, LANE), F32).at[0, :FOX_HEADS].set(b_f)
    tri = (jnp.arange(bc)[None, :] <= jnp.arange(bc)[:, None]).astype(BF16)
    H = FOX_HEADS
    hh, pp = jnp.meshgrid(jnp.arange(H), jnp.arange(3), indexing="ij")
    route = jnp.zeros((LANE, 2 * H, LANE), F32)
    route = route.at[pp * H + hh, hh, pp].set(1.0).at[pp * H + hh, H + hh, 3 + pp].set(-1.0)
    route = route.reshape(LANE, 2 * H * LANE).astype(BF16)
    const = jnp.zeros((2 * H, LANE), F32).at[:H, 3:6].set(1.0).at[H:, 0:3].set(1.0).reshape(1, 2 * H * LANE)
    tok = lambda b, i: b * nt + i
    out = jax.ShapeDtypeStruct((FOX_HEADS, T, LANE), BF16)
    aq, ak, cl = pl.pallas_call(
        _cum_kernel,
        grid=(B, nt),
        in_specs=[
            pl.BlockSpec((bc, LANE), lambda b, i: (tok(b, i), 0)),
            pl.BlockSpec((1, LANE), lambda b, i: (0, 0)),
            pl.BlockSpec((bc, bc), lambda b, i: (0, 0)),
            pl.BlockSpec((LANE, 2 * H * LANE), lambda b, i: (0, 0)),
            pl.BlockSpec((1, 2 * H * LANE), lambda b, i: (0, 0)),
        ],
        out_specs=[pl.BlockSpec((FOX_HEADS, bc, LANE), lambda b, i: (0, tok(b, i), 0))] * 2
        + [pl.BlockSpec((1, 1, LANE), lambda b, i: (tok(b, i), 0, 0))],
        out_shape=[out, out, jax.ShapeDtypeStruct((B * nt, 1, LANE), F32)],
        scratch_shapes=[pltpu.VMEM((1, LANE), F32)],
        compiler_params=_cparams(("parallel", "arbitrary")),
        name="forget_cumsum",
    )(ff, bf, tri, route, const)
    per = tq // bc
    nq = S // tq
    last = cl.reshape(B, nt, LANE)[:, per - 1::per, :FOX_HEADS]
    ncl = jnp.zeros((B, FOX_HEADS, LANE), F32).at[:, :, :nq].set(-last.transpose(0, 2, 1))
    return aq, ak, ncl.reshape(B * FOX_HEADS, 1, LANE)


FOX_UNROLL = 4
SUB = 256
BOUND_SLACK = 1.02
SKIP_MARGIN = 130.0


def _fox_kernel(ncl_ref, q_ref, aq_ref, k_ref, ak_ref, v_ref, o_ref, m_scr, acc_scr, kmax_scr, sd_scr):
    tq = q_ref.shape[1]
    S = k_ref.shape[1]
    i = pl.program_id(1)

    def token_sums(w, x):
        return lax.dot_general(w, x, (((1,), (1,)), ((), ())), preferred_element_type=F32)

    ones_row = jnp.ones((8, LANE), BF16)

    @pl.when(i == 0)
    def _():
        def body(c, mx):
            r0 = pl.multiple_of(c * tq, tq)
            kc = k_ref[0, pl.ds(r0, tq), :]
            return jnp.maximum(mx, token_sums(ones_row, kc * kc))

        mx = lax.fori_loop(0, S // tq, body, jnp.zeros((8, tq), F32))
        kmax_scr[...] = jnp.broadcast_to(jnp.sqrt(jnp.max(mx, axis=1, keepdims=True)), kmax_scr.shape)

    q = q_ref[0]
    aq = aq_ref[0]
    qa = jnp.concatenate([q, aq], axis=-1)
    ones = jnp.ones((SUB, LANE), BF16)
    nsub = tq // SUB

    def scores(j):
        out = []
        for h in range(nsub):
            r0 = pl.multiple_of(j * tq + h * SUB, SUB)
            ka = jnp.concatenate([k_ref[0, pl.ds(r0, SUB), :], ak_ref[0, pl.ds(r0, SUB), :]], axis=-1)
            out.append(lax.dot_general(qa, ka, (((1,), (1,)), ((), ())), preferred_element_type=F32))
        return out

    def update(s, j, m_prev, acc):
        mx = jnp.max(s[0], axis=1, keepdims=True)
        for sh in s[1:]:
            mx = jnp.maximum(mx, jnp.max(sh, axis=1, keepdims=True))
        m_new = jnp.broadcast_to(mx, m_scr.shape) if m_prev is None else jnp.maximum(m_prev, mx)
        m_rep = jnp.concatenate([m_new] * (SUB // LANE), axis=1)
        pv = None
        for h, sh in enumerate(s):
            r0 = pl.multiple_of(j * tq + h * SUB, SUB)
            va = jnp.concatenate([v_ref[0, pl.ds(r0, SUB), :], ones], axis=-1)
            d = jnp.dot(jnp.exp2(sh - m_rep).astype(BF16), va, preferred_element_type=F32)
            pv = d if pv is None else pv + d
        if m_prev is not None:
            alpha = jnp.exp2(m_prev - m_new)
            pv = acc * jnp.concatenate([alpha, alpha], axis=-1) + pv
        return m_new, pv

    row = lax.broadcasted_iota(jnp.int32, (tq, SUB), 0)
    col = lax.broadcasted_iota(jnp.int32, (tq, SUB), 1)
    for h, sh in enumerate(scores(i)):
        sd_scr[h] = jnp.where(col + h * SUB <= row, sh, NEG)

    def run(js, first=False):
        if first:
            s_list = [[sd_scr[h] for h in range(nsub)]] + [scores(j) for j in js[1:]]
            m, acc = None, None
        else:
            s_list = [scores(j) for j in js]
            m, acc = m_scr[...], acc_scr[...]
        for s, j in zip(s_list, js):
            m, acc = update(s, j, m, acc)
        m_scr[...] = m
        acc_scr[...] = acc

    kd = k_ref[0, pl.ds(pl.multiple_of(i * tq, tq), tq), :]
    wcol = lax.broadcasted_iota(jnp.int32, (8, 2 * LANE), 1)
    w_mc = jnp.where(wcol < LANE, 1.0, jnp.where(wcol < LANE + 3, -1.0, 0.0)).astype(BF16)
    mc = token_sums(w_mc, jnp.concatenate([q * kd, aq], axis=-1))
    qn = jnp.sqrt(token_sums(ones_row, q * q))
    r = mc - SKIP_MARGIN - BOUND_SLACK * qn * kmax_scr[:, 0:1]
    r_min = jnp.min(r, axis=1, keepdims=True)[0:1]
    lane = lax.broadcasted_iota(jnp.int32, (1, LANE), 1)
    keep = jnp.logical_and(lane < i, ncl_ref[...] >= r_min)
    n = jnp.sum(jnp.where(keep, 1.0, 0.0)).astype(jnp.int32)

    lead = jnp.minimum(n, FOX_UNROLL - 1)
    for size in range(FOX_UNROLL):

        @pl.when(lead == size)
        def _(size=size):
            run([i - u for u in range(size + 1)], first=True)

    def group(t, carry):
        j0 = i - FOX_UNROLL - t * FOX_UNROLL
        run([j0 - u for u in range(FOX_UNROLL)])
        return carry

    ng = (n - lead) // FOX_UNROLL
    lax.fori_loop(0, ng, group, 0)
    done = lead + ng * FOX_UNROLL
    size = FOX_UNROLL // 2
    while size >= 1:
        take = ((n - done) & size) != 0
        j0 = i - 1 - done

        @pl.when(take)
        def _(j0=j0, size=size):
            run([j0 - u for u in range(size)])

        done = done + jnp.where(take, size, 0)
        size //= 2
    acc = acc_scr[...]
    o_ref[...] = (acc[:, :LANE] / acc[:, LANE:]).astype(BF16)


def _fox_attention(P, Aq, Ak, ncl, B, S, tq):
    T = B * S
    nq = S // tq
    H = FOX_HEADS
    assert nq <= LANE
    return pl.pallas_call(
        _fox_kernel,
        grid=(B * H, nq),
        in_specs=[
            pl.BlockSpec((None, 1, LANE), lambda g, i: (g, 0, 0)),
            pl.BlockSpec((1, tq, LANE), lambda g, i: (G_FQ + g % H, (g // H) * nq + i, 0)),
            pl.BlockSpec((1, tq, LANE), lambda g, i: (g % H, (g // H) * nq + i, 0)),
            pl.BlockSpec((1, S, LANE), lambda g, i: (G_FK + g % H, g // H, 0)),
            pl.BlockSpec((1, S, LANE), lambda g, i: (g % H, g // H, 0)),
            pl.BlockSpec((1, S, LANE), lambda g, i: (G_FV + g % H, g // H, 0)),
        ],
        out_specs=pl.BlockSpec((tq, LANE), lambda g, i: ((g // H) * nq + i, g % H)),
        out_shape=jax.ShapeDtypeStruct((T, H * FOX_DH), BF16),
        scratch_shapes=[pltpu.VMEM((tq, LANE), F32), pltpu.VMEM((tq, 2 * LANE), F32),
                        pltpu.VMEM((8, LANE), F32), pltpu.VMEM((tq // SUB, tq, SUB), F32)],
        compiler_params=_cparams(("parallel", "arbitrary")),
        name="fox_attention",
    )(ncl, P, Aq, P, Ak, P)


def _merge_kernel(yr_ref, yl_ref, yf_ref, g0_ref, g1_ref, g2_ref, x_ref, mod_ref, gp_ref,
                  wr_ref, wl_ref, wf_ref, wo_ref, o_ref):
    D = D_MODEL

    def branch(y_ref, w_ref, g_ref):
        g = _sigmoid(_cat(g_ref, 0, 8).astype(F32))
        return g * jnp.dot(y_ref[...], w_ref[...], preferred_element_type=F32)

    m = branch(yr_ref, wr_ref, g0_ref) + branch(yl_ref, wl_ref, g1_ref) + branch(yf_ref, wf_ref, g2_ref)
    z = jnp.dot(m.astype(BF16), wo_ref[...], preferred_element_type=F32)
    o_ref[...] = x_ref[...] + mod_ref[:, 2 * D:3 * D] * _rms(z, gp_ref[...])


def _merge(y_ret, y_lru, y_fox, P, x2, mod_l, g_post, w_r, w_l, w_f, w_o, S, tm=512):
    T, D = x2.shape
    nt = S // tm
    row = pl.BlockSpec((tm, D), lambda i: (i, 0))
    wspec = pl.BlockSpec((D, D), lambda i: (0, 0))
    gate = lambda k: pl.BlockSpec((8, tm, LANE), lambda i: (G_GATE // 8 + k, i, 0))
    return pl.pallas_call(
        _merge_kernel,
        grid=(T // tm,),
        in_specs=[row, row, row, gate(0), gate(1), gate(2), row,
                  pl.BlockSpec((None, 1, 6 * D), lambda i: (i // nt, 0, 0)),
                  pl.BlockSpec((1, D), lambda i: (0, 0)),
                  wspec, wspec, wspec, wspec],
        out_specs=row,
        out_shape=jax.ShapeDtypeStruct((T, D), F32),
        compiler_params=_cparams(("parallel",)),
        name="merge_out_proj",
    )(y_ret, y_lru, y_fox, P, P, P, x2, mod_l, g_post.reshape(1, D), w_r, w_l, w_f, w_o)


def _mlp_kernel(x_ref, mod_ref, gpre_ref, gpost_ref, w1_ref, w2_ref, o_ref):
    D = D_MODEL
    x = x_ref[...]
    h = (_rms(x, gpre_ref[...]) * (1.0 + mod_ref[:, 4 * D:5 * D]) + mod_ref[:, 3 * D:4 * D]).astype(BF16)
    f = jnp.zeros(x.shape, F32)
    ck = 1024
    for c in range(D_FF // ck):
        a = jnp.maximum(jnp.dot(h, w1_ref[:, c * ck:(c + 1) * ck], preferred_element_type=F32), 0.0)
        f = f + jnp.dot((a * a).astype(BF16), w2_ref[c * ck:(c + 1) * ck, :], preferred_element_type=F32)
    o_ref[...] = x + mod_ref[:, 5 * D:6 * D] * _rms(f, gpost_ref[...])


def _mlp(x2, mod_l, g_pre, g_post, w1, w2, S, tm=512):
    T, D = x2.shape
    nt = S // tm
    row = pl.BlockSpec((tm, D), lambda i: (i, 0))
    vec = pl.BlockSpec((1, D), lambda i: (0, 0))
    return pl.pallas_call(
        _mlp_kernel,
        grid=(T // tm,),
        in_specs=[row, pl.BlockSpec((None, 1, 6 * D), lambda i: (i // nt, 0, 0)), vec, vec,
                  pl.BlockSpec((D, D_FF), lambda i: (0, 0)), pl.BlockSpec((D_FF, D), lambda i: (0, 0))],
        out_specs=row,
        out_shape=jax.ShapeDtypeStruct((T, D), F32),
        compiler_params=_cparams(("parallel",)),
        name="relu2_mlp",
    )(x2, mod_l, g_pre.reshape(1, D), g_post.reshape(1, D), w1, w2)


def _prep_w_in(w_in):
    rk0, rk1 = G_RK * LANE, G_RV * LANE
    fq0, fq1 = G_FQ * LANE, G_FK * LANE
    parts = [w_in[:, :rk0], w_in[:, rk0:rk1] * RET_DK ** -0.5, w_in[:, rk1:fq0],
             w_in[:, fq0:fq1] * (FOX_DH ** -0.5 * LOG2E), w_in[:, fq1:FF_COL], w_in[:, FF_COL + FOX_HEADS:]]
    w_main = jnp.concatenate([p.astype(BF16) for p in parts], axis=1)
    w_ff = jnp.zeros((w_in.shape[0], LANE), BF16).at[:, :FOX_HEADS].set(
        w_in[:, FF_COL:FF_COL + FOX_HEADS].astype(BF16))
    return w_main, w_ff


def kernel(x, c, ada_w, ada_b, norm_pre_mix, norm_post_mix, norm_pre_mlp, norm_post_mlp, w_in, conv_w, conv_b,
           lru_w_a, lru_b_a, lru_w_x, lru_b_x, lru_lambda, fox_b_f, ret_w_o, lru_w_o, fox_w_o, w_out, mlp_w1,
           mlp_w2):
    B, S, D = x.shape
    L = ada_w.shape[0]
    assert D == D_MODEL and S % 1024 == 0
    ret_bt, lru_bt, fox_tq = 256, 256, 512
    mod = _modulation(c, ada_w, ada_b)
    tables = _retention_tables(S, ret_bt)
    x2 = x.reshape(B * S, D)
    for l in range(L):
        w_main, w_ff = _prep_w_in(w_in[l])
        P, ff = _in_projection(x2, mod[l], norm_pre_mix[l], w_main, w_ff, S)
        y_ret = _retention(P, tables, B, S, ret_bt)
        w_ax = (jnp.concatenate([lru_w_a[l], lru_w_x[l]], axis=-1) * -LOG2E).astype(BF16)
        y_lru = _rg_lru(P, conv_w[l], conv_b[l], w_ax, lru_b_a[l] * -LOG2E, lru_b_x[l] * -LOG2E,
                        lru_lambda[l], B, S, lru_bt)
        Aq, Ak, ncl = _forget_cumsum(ff, fox_b_f[l], B, S, fox_tq)
        y_fox = _fox_attention(P, Aq, Ak, ncl, B, S, fox_tq)
        x2 = _merge(y_ret, y_lru, y_fox, P, x2, mod[l], norm_post_mix[l], ret_w_o[l].astype(BF16),
                    lru_w_o[l].astype(BF16), fox_w_o[l].astype(BF16), w_out[l].astype(BF16), S)
        x2 = _mlp(x2, mod[l], norm_pre_mlp[l], norm_post_mlp[l], mlp_w1[l].astype(BF16),
                  mlp_w2[l].astype(BF16), S)
    return x2.reshape(B, S, D)
```

```python
import functools

import jax
import jax.numpy as jnp
from jax import lax
from jax.experimental import pallas as pl
from jax.experimental.pallas import tpu as pltpu

F32 = jnp.float32
BF16 = jnp.bfloat16

D_MODEL = 1024
EPS = 1e-6
ROPE_BASE = 10000.0
CHUNK = 64
RET_HEADS, RET_DK, RET_DV = 4, 128, 256
LRU_BLOCKS, LRU_BW, CONV_W, LRU_C = 8, 128, 4, 8.0
FOX_HEADS, FOX_DH = 8, 128
D_FF = 4 * D_MODEL
LANE = 128

G_RQ, G_RK, G_RV, G_RG, G_LX, G_LY, G_FQ, G_FK, G_FV, G_GATE = 0, 4, 8, 16, 24, 32, 40, 48, 56, 64
N_GROUPS = 88
FF_COL = 2 * RET_HEADS * RET_DK + 2 * RET_HEADS * RET_DV + 2 * D_MODEL + 3 * FOX_HEADS * FOX_DH

NEG = -1e30
F32_TINY = 1.1754944e-38
LOG2E = 1.4426950408889634
VMEM_LIMIT = 56 * 1024 * 1024


def _cparams(sem):
    return pltpu.CompilerParams(dimension_semantics=sem, vmem_limit_bytes=VMEM_LIMIT)


def _sigmoid(z):
    return 1.0 / (1.0 + jnp.exp(-z))


def _softplus(z):
    return jnp.maximum(z, 0.0) + jnp.log1p(jnp.exp(-jnp.abs(z)))


def _rms(x, g):
    return x * lax.rsqrt(jnp.mean(x * x, axis=-1, keepdims=True) + EPS) * g


def _cat(ref, lo, n):
    return jnp.concatenate([ref[lo + k] for k in range(n)], axis=-1)


def _mod_kernel(c_ref, w_ref, b_ref, o_ref):
    c = c_ref[...]
    o_ref[0] = jnp.dot(c * _sigmoid(c), w_ref[0], preferred_element_type=F32,
                       precision=lax.Precision.HIGHEST) + b_ref[0]


def _modulation(c, ada_w, ada_b):
    L, D, D6 = ada_w.shape
    B = c.shape[0]
    c8 = jnp.zeros((8, D), F32).at[:B].set(c)
    tn = 1024
    out = pl.pallas_call(
        _mod_kernel,
        grid=(L, D6 // tn),
        in_specs=[
            pl.BlockSpec((8, D), lambda l, j: (0, 0)),
            pl.BlockSpec((1, D, tn), lambda l, j: (l, 0, j)),
            pl.BlockSpec((1, 1, tn), lambda l, j: (l, 0, j)),
        ],
        out_specs=pl.BlockSpec((1, 8, tn), lambda l, j: (l, 0, j)),
        out_shape=jax.ShapeDtypeStruct((L, 8, D6), F32),
        compiler_params=_cparams(("parallel", "parallel")),
        name="adaln_mod",
    )(c8, ada_w, ada_b.reshape(L, 1, D6))
    return out[:, :B].reshape(L, B, 1, D6)


def _inproj_kernel(x_ref, mod_ref, g_ref, w_ref, wff_ref, p_ref, ff_ref, h_scr):
    D = D_MODEL

    @pl.when(pl.program_id(1) == 0)
    def _():
        h = _rms(x_ref[...], g_ref[...]) * (1.0 + mod_ref[:, D:2 * D]) + mod_ref[:, 0:D]
        hb = h.astype(BF16)
        h_scr[...] = hb
        ff_ref[...] = jnp.dot(hb, wff_ref[...], preferred_element_type=F32)

    acc = jnp.dot(h_scr[...], w_ref[...], preferred_element_type=F32)
    for g in range(acc.shape[1] // LANE):
        p_ref[g] = acc[:, g * LANE:(g + 1) * LANE].astype(BF16)


def _in_projection(x2, mod_l, g_pre, w_main, w_ff, S, tm=2048, tn=1024):
    T, D = x2.shape
    n_main = w_main.shape[1]
    nt = S // tm
    return pl.pallas_call(
        _inproj_kernel,
        grid=(T // tm, n_main // tn),
        in_specs=[
            pl.BlockSpec((tm, D), lambda i, j: (i, 0)),
            pl.BlockSpec((None, 1, 6 * D), lambda i, j: (i // nt, 0, 0)),
            pl.BlockSpec((1, D), lambda i, j: (0, 0)),
            pl.BlockSpec((D, tn), lambda i, j: (0, j)),
            pl.BlockSpec((D, LANE), lambda i, j: (0, 0)),
        ],
        out_specs=[
            pl.BlockSpec((tn // LANE, tm, LANE), lambda i, j: (j, i, 0)),
            pl.BlockSpec((tm, LANE), lambda i, j: (i, 0)),
        ],
        out_shape=[
            jax.ShapeDtypeStruct((n_main // LANE, T, LANE), BF16),
            jax.ShapeDtypeStruct((T, LANE), F32),
        ],
        scratch_shapes=[pltpu.VMEM((tm, D), BF16)],
        compiler_params=_cparams(("parallel", "arbitrary")),
        name="in_proj",
    )(x2, mod_l, g_pre.reshape(1, D), w_main, w_ff)


def _ret_kernel(q_ref, k_ref, v_ref, rg_ref, cos_ref, sin_ref, dmat_ref, qdec_ref, kdec_ref, cdec_ref,
                o_ref, state):
    @pl.when(pl.program_id(1) == 0)
    def _():
        state[...] = jnp.zeros_like(state)

    cos = cos_ref[...]
    sin = sin_ref[...]
    half = RET_DK // 2
    for h in range(RET_HEADS):
        q = q_ref[h].astype(F32)
        k = k_ref[h].astype(F32)
        qr = q * cos + pltpu.roll(q, half, 1) * sin
        kr = k * cos + pltpu.roll(k, half, 1) * sin
        v = _cat(v_ref, 2 * h, 2)
        s = lax.dot_general(qr.astype(BF16), kr.astype(BF16), (((1,), (1,)), ((), ())),
                            preferred_element_type=F32) * dmat_ref[h]
        st = state[h]
        o = jnp.dot(s.astype(BF16), v, preferred_element_type=F32)
        o = o + jnp.dot((qr * qdec_ref[h]).astype(BF16), st.astype(BF16), preferred_element_type=F32)
        kd = (kr * kdec_ref[h]).astype(BF16)
        state[h] = st * cdec_ref[h] + lax.dot_general(kd, v, (((0,), (0,)), ((), ())),
                                                      preferred_element_type=F32)
        d = o - jnp.mean(o, axis=-1, keepdims=True)
        on = d * lax.rsqrt(jnp.mean(d * d, axis=-1, keepdims=True) + EPS)
        g = _cat(rg_ref, 2 * h, 2).astype(F32)
        o_ref[:, h * RET_DV:(h + 1) * RET_DV] = (g * _sigmoid(g) * on).astype(BF16)


def _retention_tables(S, bt):
    H = RET_HEADS
    log_g = jnp.log(1.0 - 2.0 ** (-5.0 - jnp.arange(H, dtype=F32)))
    idx = jnp.arange(bt, dtype=F32)
    dist = jnp.abs(idx[:, None] - idx[None, :])
    ch = jnp.arange(bt) // CHUNK
    allowed = ch[None, :] <= ch[:, None]
    dmat = jnp.where(allowed[None], jnp.exp(dist[None] * log_g[:, None, None]), 0.0)
    qdec = jnp.exp((idx + 1.0)[None, :] * log_g[:, None])
    kdec = jnp.exp((bt - 1.0 - idx)[None, :] * log_g[:, None])
    cdec = jnp.exp(bt * log_g)
    qdec = jnp.broadcast_to(qdec[:, :, None], (H, bt, LANE))
    kdec = jnp.broadcast_to(kdec[:, :, None], (H, bt, LANE))
    cdec = jnp.broadcast_to(cdec[:, None, None], (H, 1, RET_DV))
    half = RET_DK // 2
    inv = ROPE_BASE ** (-jnp.arange(half, dtype=F32) / half)
    ang = jnp.arange(S).astype(F32)[:, None] * inv[None, :]
    cos, sin = jnp.cos(ang), jnp.sin(ang)
    cos2 = jnp.concatenate([cos, cos], axis=-1)
    sin2 = jnp.concatenate([-sin, sin], axis=-1)
    return cos2, sin2, dmat, qdec, kdec, cdec


def _retention(P, tables, B, S, bt):
    T = B * S
    nt = S // bt
    cos2, sin2, dmat, qdec, kdec, cdec = tables
    H = RET_HEADS
    tok = lambda b, i: b * nt + i
    return pl.pallas_call(
        _ret_kernel,
        grid=(B, nt),
        in_specs=[
            pl.BlockSpec((4, bt, LANE), lambda b, i: (G_RQ // 4, tok(b, i), 0)),
            pl.BlockSpec((4, bt, LANE), lambda b, i: (G_RK // 4, tok(b, i), 0)),
            pl.BlockSpec((8, bt, LANE), lambda b, i: (G_RV // 8, tok(b, i), 0)),
            pl.BlockSpec((8, bt, LANE), lambda b, i: (G_RG // 8, tok(b, i), 0)),
            pl.BlockSpec((bt, LANE), lambda b, i: (i, 0)),
            pl.BlockSpec((bt, LANE), lambda b, i: (i, 0)),
            pl.BlockSpec((H, bt, bt), lambda b, i: (0, 0, 0)),
            pl.BlockSpec((H, bt, LANE), lambda b, i: (0, 0, 0)),
            pl.BlockSpec((H, bt, LANE), lambda b, i: (0, 0, 0)),
            pl.BlockSpec((H, 1, RET_DV), lambda b, i: (0, 0, 0)),
        ],
        out_specs=pl.BlockSpec((bt, H * RET_DV), lambda b, i: (tok(b, i), 0)),
        out_shape=jax.ShapeDtypeStruct((T, H * RET_DV), BF16),
        scratch_shapes=[pltpu.VMEM((H, RET_DK, RET_DV), F32)],
        compiler_params=_cparams(("parallel", "arbitrary")),
        name="retention",
    )(P, P, P, P, cos2, sin2, dmat, qdec, kdec, cdec)


def _lru_kernel(lx_ref, ly_ref, cw_ref, cb_ref, wax_ref, ba_ref, bx_ref, lam_ref, o_ref,
                tail, hprev, a_scr, b_scr, h_scr):
    bt = lx_ref.shape[1]

    @pl.when(pl.program_id(1) == 0)
    def _():
        tail[...] = jnp.zeros_like(tail)
        hprev[...] = jnp.zeros_like(hprev)

    r8 = lax.broadcasted_iota(jnp.int32, (bt // 8, 8, LANE), 1)
    for n in range(LRU_BLOCKS):
        sl = slice(n * LANE, (n + 1) * LANE)
        x = lx_ref[n].astype(F32)
        xp = jnp.concatenate([tail[:, sl], x], axis=0)
        u = cb_ref[:, sl] + x * cw_ref[CONV_W - 1:CONV_W, sl]
        for k in range(1, CONV_W):
            u = u + pltpu.roll(xp, k, 0)[8:] * cw_ref[CONV_W - 1 - k:CONV_W - k, sl]
        tail[:, sl] = x[bt - 8:bt]
        gates = jnp.dot(u.astype(BF16), wax_ref[n], preferred_element_type=F32)
        r = 1.0 / (1.0 + jnp.exp2(gates[:, :LANE] + ba_ref[:, sl]))
        ig = 1.0 / (1.0 + jnp.exp2(gates[:, LANE:] + bx_ref[:, sl]))
        a = jnp.exp2(r * ((-LRU_C * LOG2E) * _softplus(-lam_ref[:, sl])))
        v = jnp.maximum(1.0 - a * a, 0.0)
        bb = v * lax.rsqrt(jnp.maximum(v, F32_TINY)) * (ig * u)
        a = a.reshape(bt // 8, 8, LANE)
        bb = bb.reshape(bt // 8, 8, LANE)
        for d in (1, 2, 4):
            m = r8 >= d
            a_sh = jnp.where(m, pltpu.roll(a, d, 1), 1.0)
            b_sh = jnp.where(m, pltpu.roll(bb, d, 1), 0.0)
            bb = a * b_sh + bb
            a = a * a_sh
        a_scr[:, sl] = a.reshape(bt, LANE)
        b_scr[:, sl] = bb.reshape(bt, LANE)

    def body(c, hp):
        r0 = pl.multiple_of(c * 8, 8)
        h = a_scr[pl.ds(r0, 8), :] * hp + b_scr[pl.ds(r0, 8), :]
        h_scr[pl.ds(r0, 8), :] = h
        return jnp.broadcast_to(h[7:8, :], h.shape)

    hprev[...] = lax.fori_loop(0, bt // 8, body, hprev[...], unroll=4)
    for n in range(LRU_BLOCKS):
        sl = slice(n * LANE, (n + 1) * LANE)
        o_ref[:, sl] = (jax.nn.gelu(ly_ref[n].astype(F32)) * h_scr[:, sl]).astype(BF16)


def _rg_lru(P, conv_w, conv_b, w_ax, b_a, b_x, lam, B, S, bt):
    T = B * S
    nt = S // bt
    W = LRU_BLOCKS * LRU_BW
    tok = lambda b, i: b * nt + i
    full = lambda shape: pl.BlockSpec(shape, lambda b, i: (0,) * len(shape))
    return pl.pallas_call(
        _lru_kernel,
        grid=(B, nt),
        in_specs=[
            pl.BlockSpec((8, bt, LANE), lambda b, i: (G_LX // 8, tok(b, i), 0)),
            pl.BlockSpec((8, bt, LANE), lambda b, i: (G_LY // 8, tok(b, i), 0)),
            full((CONV_W, W)), full((1, W)), full((LRU_BLOCKS, LRU_BW, 2 * LRU_BW)),
            full((1, W)), full((1, W)), full((1, W)),
        ],
        out_specs=pl.BlockSpec((bt, W), lambda b, i: (tok(b, i), 0)),
        out_shape=jax.ShapeDtypeStruct((T, W), BF16),
        scratch_shapes=[pltpu.VMEM((8, W), F32), pltpu.VMEM((8, W), F32),
                        pltpu.VMEM((bt, W), F32), pltpu.VMEM((bt, W), F32), pltpu.VMEM((bt, W), F32)],
        compiler_params=_cparams(("parallel", "arbitrary")),
        name="rg_lru",
    )(P, P, conv_w, conv_b.reshape(1, W), w_ax, b_a.reshape(1, W), b_x.reshape(1, W), lam.reshape(1, W))


def _cum_kernel(ff_ref, bf_ref, tri_ref, route_ref, const_ref, aq_ref, ak_ref, cl_ref, carry):
    bc = ff_ref.shape[0]

    @pl.when(pl.program_id(1) == 0)
    def _():
        carry[...] = jnp.zeros_like(carry)

    z = ff_ref[...] + bf_ref[...]
    logf = jnp.minimum(z, 0.0) - jnp.log1p(jnp.exp(-jnp.abs(z)))
    l1 = logf.astype(BF16)
    e1 = logf - l1.astype(F32)
    l2 = e1.astype(BF16)
    l3 = (e1 - l2.astype(F32)).astype(BF16)
    cs = jnp.dot(tri_ref[...], jnp.concatenate([l1, l2, l3], axis=1), preferred_element_type=F32)
    cum = (cs[:, :LANE] + cs[:, LANE:2 * LANE]) + cs[:, 2 * LANE:] + carry[...]
    carry[...] = cum[bc - 1:bc]
    cum = cum * LOG2E
    cl_ref[0] = cum[bc - 1:bc]
    hi = cum.astype(BF16).astype(F32)
    r1 = cum - hi
    mid = r1.astype(BF16).astype(F32)
    lo = r1 - mid
    lane = lax.broadcasted_iota(jnp.int32, (bc, LANE), 1)
    H = FOX_HEADS
    pieces = jnp.where(lane < H, hi, jnp.where(lane < 2 * H, pltpu.roll(mid, H, 1), pltpu.roll(lo, 2 * H, 1)))
    out = jnp.dot(pieces.astype(BF16), route_ref[...], preferred_element_type=F32) + const_ref[...]
    for h in range(H):
        aq_ref[h] = out[:, h * LANE:(h + 1) * LANE].astype(BF16)
        ak_ref[h] = out[:, (H + h) * LANE:(H + h + 1) * LANE].astype(BF16)


def _forget_cumsum(ff, b_f, B, S, tq, bc=256):
    T = B * S
    nt = S // bc
    bf = jnp.zeros((1, LANE), F32).at[0, :FOX_HEADS].set(b_f)
    tri = (jnp.arange(bc)[None, :] <= jnp.arange(bc)[:, None]).astype(BF16)
    H = FOX_HEADS
    hh, pp = jnp.meshgrid(jnp.arange(H), jnp.arange(3), indexing="ij")
    route = jnp.zeros((LANE, 2 * H, LANE), F32)
    route = route.at[pp * H + hh, hh, pp].set(1.0).at[pp * H + hh, H + hh, 3 + pp].set(-1.0)
    route = route.reshape(LANE, 2 * H * LANE).astype(BF16)
    const = jnp.zeros((2 * H, LANE), F32).at[:H, 3:6].set(1.0).at[H:, 0:3].set(1.0).reshape(1, 2 * H * LANE)
    tok = lambda b, i: b * nt + i
    out = jax.ShapeDtypeStruct((FOX_HEADS, T, LANE), BF16)
    aq, ak, cl = pl.pallas_call(
        _cum_kernel,
        grid=(B, nt),
        in_specs=[
            pl.BlockSpec((bc, LANE), lambda b, i: (tok(b, i), 0)),
            pl.BlockSpec((1, LANE), lambda b, i: (0, 0)),
            pl.BlockSpec((bc, bc), lambda b, i: (0, 0)),
            pl.BlockSpec((LANE, 2 * H * LANE), lambda b, i: (0, 0)),
            pl.BlockSpec((1, 2 * H * LANE), lambda b, i: (0, 0)),
        ],
        out_specs=[pl.BlockSpec((FOX_HEADS, bc, LANE), lambda b, i: (0, tok(b, i), 0))] * 2
        + [pl.BlockSpec((1, 1, LANE), lambda b, i: (tok(b, i), 0, 0))],
        out_shape=[out, out, jax.ShapeDtypeStruct((B * nt, 1, LANE), F32)],
        scratch_shapes=[pltpu.VMEM((1, LANE), F32)],
        compiler_params=_cparams(("parallel", "arbitrary")),
        name="forget_cumsum",
    )(ff, bf, tri, route, const)
    per = tq // bc
    nq = S // tq
    last = cl.reshape(B, nt, LANE)[:, per - 1::per, :FOX_HEADS]
    ncl = jnp.zeros((B, FOX_HEADS, LANE), F32).at[:, :, :nq].set(-last.transpose(0, 2, 1))
    return aq, ak, ncl.reshape(B * FOX_HEADS, 1, LANE)


FOX_UNROLL = 4
SUB = 256
BOUND_SLACK = 1.02
SKIP_MARGIN = 130.0


def _fox_kernel(ncl_ref, q_ref, aq_ref, qn_ref, aqn_ref, k_ref, ak_ref, v_ref, o_ref,
                m_scr, acc_scr, kmax_scr, n_smem):
    tq = q_ref.shape[1]
    S = k_ref.shape[1]
    i = pl.program_id(1)

    def token_sums(w, x):
        return lax.dot_general(w, x, (((1,), (1,)), ((), ())), preferred_element_type=F32)

    ones_row = jnp.ones((8, LANE), BF16)

    @pl.when(i == 0)
    def _():
        def body(c, mx):
            r0 = pl.multiple_of(c * tq, tq)
            kc = k_ref[0, pl.ds(r0, tq), :]
            return jnp.maximum(mx, token_sums(ones_row, kc * kc))

        mx = lax.fori_loop(0, S // tq, body, jnp.zeros((8, tq), F32))
        kmax_scr[...] = jnp.broadcast_to(jnp.sqrt(jnp.max(mx, axis=1, keepdims=True)), kmax_scr.shape)

    q = q_ref[0]
    aq = aq_ref[0]
    qa = jnp.concatenate([q, aq], axis=-1)
    ones = jnp.ones((SUB, LANE), BF16)
    nsub = tq // SUB

    def scores(j):
        out = []
        for h in range(nsub):
            r0 = pl.multiple_of(j * tq + h * SUB, SUB)
            ka = jnp.concatenate([k_ref[0, pl.ds(r0, SUB), :], ak_ref[0, pl.ds(r0, SUB), :]], axis=-1)
            out.append(lax.dot_general(qa, ka, (((1,), (1,)), ((), ())), preferred_element_type=F32))
        return out

    def update(s, j, m_prev, acc):
        mx = jnp.max(s[0], axis=1, keepdims=True)
        for sh in s[1:]:
            mx = jnp.maximum(mx, jnp.max(sh, axis=1, keepdims=True))
        m_new = jnp.broadcast_to(mx, m_scr.shape) if m_prev is None else jnp.maximum(m_prev, mx)
        m_rep = jnp.concatenate([m_new] * (SUB // LANE), axis=1)
        pv = None
        for h, sh in enumerate(s):
            r0 = pl.multiple_of(j * tq + h * SUB, SUB)
            va = jnp.concatenate([v_ref[0, pl.ds(r0, SUB), :], ones], axis=-1)
            d = jnp.dot(jnp.exp2(sh - m_rep).astype(BF16), va, preferred_element_type=F32)
            pv = d if pv is None else pv + d
        if m_prev is not None:
            alpha = jnp.exp2(m_prev - m_new)
            pv = acc * jnp.concatenate([alpha, alpha], axis=-1) + pv
        return m_new, pv

    row = lax.broadcasted_iota(jnp.int32, (tq, SUB), 0)
    col = lax.broadcasted_iota(jnp.int32, (tq, SUB), 1)

    def run(js, first=False):
        s_list = [scores(j) for j in js]
        if first:
            s_list[0] = [jnp.where(col + h * SUB <= row, sh, NEG) for h, sh in enumerate(s_list[0])]
            m, acc = None, None
        else:
            m, acc = m_scr[...], acc_scr[...]
        for s, j in zip(s_list, js):
            m, acc = update(s, j, m, acc)
        m_scr[...] = m
        acc_scr[...] = acc

    def next_walk_length():
        i1 = jnp.minimum(i + 1, pl.num_programs(1) - 1)
        qn, aqn = qn_ref[0], aqn_ref[0]
        kd = k_ref[0, pl.ds(pl.multiple_of(i1 * tq, tq), tq), :]
        wcol = lax.broadcasted_iota(jnp.int32, (8, 2 * LANE), 1)
        w_mc = jnp.where(wcol < LANE, 1.0, jnp.where(wcol < LANE + 3, -1.0, 0.0)).astype(BF16)
        mc = token_sums(w_mc, jnp.concatenate([qn * kd, aqn], axis=-1))
        qnorm = jnp.sqrt(token_sums(ones_row, qn * qn))
        r = mc - SKIP_MARGIN - BOUND_SLACK * qnorm * kmax_scr[:, 0:1]
        r_min = jnp.min(r, axis=1, keepdims=True)[0:1]
        lane = lax.broadcasted_iota(jnp.int32, (1, LANE), 1)
        keep = jnp.logical_and(lane < i1, ncl_ref[...] >= r_min)
        n_smem[0] = jnp.sum(jnp.where(keep, 1.0, 0.0)).astype(jnp.int32)

    n = jnp.where(i == 0, 0, n_smem[0])

    lead = jnp.minimum(n, FOX_UNROLL - 1)
    for size in range(FOX_UNROLL):

        @pl.when(lead == size)
        def _(size=size):
            next_walk_length()
            run([i - u for u in range(size + 1)], first=True)

    def group(t, carry):
        j0 = i - FOX_UNROLL - t * FOX_UNROLL
        run([j0 - u for u in range(FOX_UNROLL)])
        return carry

    ng = (n - lead) // FOX_UNROLL
    lax.fori_loop(0, ng, group, 0)
    done = lead + ng * FOX_UNROLL
    size = FOX_UNROLL // 2
    while size >= 1:
        take = ((n - done) & size) != 0
        j0 = i - 1 - done

        @pl.when(take)
        def _(j0=j0, size=size):
            run([j0 - u for u in range(size)])

        done = done + jnp.where(take, size, 0)
        size //= 2
    acc = acc_scr[...]
    o_ref[...] = (acc[:, :LANE] / acc[:, LANE:]).astype(BF16)


def _fox_attention(P, Aq, Ak, ncl, B, S, tq):
    T = B * S
    nq = S // tq
    H = FOX_HEADS
    assert nq <= LANE
    nxt = lambda i: jnp.minimum(i + 1, nq - 1)
    return pl.pallas_call(
        _fox_kernel,
        grid=(B * H, nq),
        in_specs=[
            pl.BlockSpec((None, 1, LANE), lambda g, i: (g, 0, 0)),
            pl.BlockSpec((1, tq, LANE), lambda g, i: (G_FQ + g % H, (g // H) * nq + i, 0)),
            pl.BlockSpec((1, tq, LANE), lambda g, i: (g % H, (g // H) * nq + i, 0)),
            pl.BlockSpec((1, tq, LANE), lambda g, i: (G_FQ + g % H, (g // H) * nq + nxt(i), 0)),
            pl.BlockSpec((1, tq, LANE), lambda g, i: (g % H, (g // H) * nq + nxt(i), 0)),
            pl.BlockSpec((1, S, LANE), lambda g, i: (G_FK + g % H, g // H, 0)),
            pl.BlockSpec((1, S, LANE), lambda g, i: (g % H, g // H, 0)),
            pl.BlockSpec((1, S, LANE), lambda g, i: (G_FV + g % H, g // H, 0)),
        ],
        out_specs=pl.BlockSpec((tq, LANE), lambda g, i: ((g // H) * nq + i, g % H)),
        out_shape=jax.ShapeDtypeStruct((T, H * FOX_DH), BF16),
        scratch_shapes=[pltpu.VMEM((tq, LANE), F32), pltpu.VMEM((tq, 2 * LANE), F32),
                        pltpu.VMEM((8, LANE), F32), pltpu.SMEM((1,), jnp.int32)],
        compiler_params=_cparams(("parallel", "arbitrary")),
        name="fox_attention",
    )(ncl, P, Aq, P, Aq, P, Ak, P)


def _merge_kernel(yr_ref, yl_ref, yf_ref, g0_ref, g1_ref, g2_ref, x_ref, mod_ref, gp_ref,
                  wr_ref, wl_ref, wf_ref, wo_ref, o_ref):
    D = D_MODEL

    def branch(y_ref, w_ref, g_ref):
        g = _sigmoid(_cat(g_ref, 0, 8).astype(F32))
        return g * jnp.dot(y_ref[...], w_ref[...], preferred_element_type=F32)

    m = branch(yr_ref, wr_ref, g0_ref) + branch(yl_ref, wl_ref, g1_ref) + branch(yf_ref, wf_ref, g2_ref)
    z = jnp.dot(m.astype(BF16), wo_ref[...], preferred_element_type=F32)
    o_ref[...] = x_ref[...] + mod_ref[:, 2 * D:3 * D] * _rms(z, gp_ref[...])


def _merge(y_ret, y_lru, y_fox, P, x2, mod_l, g_post, w_r, w_l, w_f, w_o, S, tm=512):
    T, D = x2.shape
    nt = S // tm
    row = pl.BlockSpec((tm, D), lambda i: (i, 0))
    wspec = pl.BlockSpec((D, D), lambda i: (0, 0))
    gate = lambda k: pl.BlockSpec((8, tm, LANE), lambda i: (G_GATE // 8 + k, i, 0))
    return pl.pallas_call(
        _merge_kernel,
        grid=(T // tm,),
        in_specs=[row, row, row, gate(0), gate(1), gate(2), row,
                  pl.BlockSpec((None, 1, 6 * D), lambda i: (i // nt, 0, 0)),
                  pl.BlockSpec((1, D), lambda i: (0, 0)),
                  wspec, wspec, wspec, wspec],
        out_specs=row,
        out_shape=jax.ShapeDtypeStruct((T, D), F32),
        compiler_params=_cparams(("parallel",)),
        name="merge_out_proj",
    )(y_ret, y_lru, y_fox, P, P, P, x2, mod_l, g_post.reshape(1, D), w_r, w_l, w_f, w_o)


def _mlp_kernel(x_ref, mod_ref, gpre_ref, gpost_ref, w1_ref, w2_ref, o_ref):
    D = D_MODEL
    x = x_ref[...]
    h = (_rms(x, gpre_ref[...]) * (1.0 + mod_ref[:, 4 * D:5 * D]) + mod_ref[:, 3 * D:4 * D]).astype(BF16)
    f = jnp.zeros(x.shape, F32)
    ck = 1024
    for c in range(D_FF // ck):
        a = jnp.maximum(jnp.dot(h, w1_ref[:, c * ck:(c + 1) * ck], preferred_element_type=F32), 0.0)
        f = f + jnp.dot((a * a).astype(BF16), w2_ref[c * ck:(c + 1) * ck, :], preferred_element_type=F32)
    o_ref[...] = x + mod_ref[:, 5 * D:6 * D] * _rms(f, gpost_ref[...])


def _mlp(x2, mod_l, g_pre, g_post, w1, w2, S, tm=512):
    T, D = x2.shape
    nt = S // tm
    row = pl.BlockSpec((tm, D), lambda i: (i, 0))
    vec = pl.BlockSpec((1, D), lambda i: (0, 0))
    return pl.pallas_call(
        _mlp_kernel,
        grid=(T // tm,),
        in_specs=[row, pl.BlockSpec((None, 1, 6 * D), lambda i: (i // nt, 0, 0)), vec, vec,
                  pl.BlockSpec((D, D_FF), lambda i: (0, 0)), pl.BlockSpec((D_FF, D), lambda i: (0, 0))],
        out_specs=row,
        out_shape=jax.ShapeDtypeStruct((T, D), F32),
        compiler_params=_cparams(("parallel",)),
        name="relu2_mlp",
    )(x2, mod_l, g_pre.reshape(1, D), g_post.reshape(1, D), w1, w2)


def _prep_w_in(w_in):
    rk0, rk1 = G_RK * LANE, G_RV * LANE
    fq0, fq1 = G_FQ * LANE, G_FK * LANE
    parts = [w_in[:, :rk0], w_in[:, rk0:rk1] * RET_DK ** -0.5, w_in[:, rk1:fq0],
             w_in[:, fq0:fq1] * (FOX_DH ** -0.5 * LOG2E), w_in[:, fq1:FF_COL], w_in[:, FF_COL + FOX_HEADS:]]
    w_main = jnp.concatenate([p.astype(BF16) for p in parts], axis=1)
    w_ff = jnp.zeros((w_in.shape[0], LANE), BF16).at[:, :FOX_HEADS].set(
        w_in[:, FF_COL:FF_COL + FOX_HEADS].astype(BF16))
    return w_main, w_ff


def kernel(x, c, ada_w, ada_b, norm_pre_mix, norm_post_mix, norm_pre_mlp, norm_post_mlp, w_in, conv_w, conv_b,
           lru_w_a, lru_b_a, lru_w_x, lru_b_x, lru_lambda, fox_b_f, ret_w_o, lru_w_o, fox_w_o, w_out, mlp_w1,
           mlp_w2):
    B, S, D = x.shape
    L = ada_w.shape[0]
    assert D == D_MODEL and S % 2048 == 0
    ret_bt, lru_bt, fox_tq = 256, 256, 512
    mod = _modulation(c, ada_w, ada_b)
    tables = _retention_tables(S, ret_bt)
    x2 = x.reshape(B * S, D)
    for l in range(L):
        w_main, w_ff = _prep_w_in(w_in[l])
        P, ff = _in_projection(x2, mod[l], norm_pre_mix[l], w_main, w_ff, S)
        y_ret = _retention(P, tables, B, S, ret_bt)
        w_ax = (jnp.concatenate([lru_w_a[l], lru_w_x[l]], axis=-1) * -LOG2E).astype(BF16)
        y_lru = _rg_lru(P, conv_w[l], conv_b[l], w_ax, lru_b_a[l] * -LOG2E, lru_b_x[l] * -LOG2E,
                        lru_lambda[l], B, S, lru_bt)
        Aq, Ak, ncl = _forget_cumsum(ff, fox_b_f[l], B, S, fox_tq)
        y_fox = _fox_attention(P, Aq, Ak, ncl, B, S, fox_tq)
        x2 = _merge(y_ret, y_lru, y_fox, P, x2, mod[l], norm_post_mix[l], ret_w_o[l].astype(BF16),
                    lru_w_o[l].astype(BF16), fox_w_o[l].astype(BF16), w_out[l].astype(BF16), S)
        x2 = _mlp(x2, mod[l], norm_pre_mlp[l], norm_post_mlp[l], mlp_w1[l].astype(BF16),
                  mlp_w2[l].astype(BF16), S)
    return x2.reshape(B, S, D)
```

```python
import functools

import jax
import jax.numpy as jnp
from jax import lax
from jax.experimental import pallas as pl
from jax.experimental.pallas import tpu as pltpu

F32 = jnp.float32
BF16 = jnp.bfloat16

D_MODEL = 1024
EPS = 1e-6
ROPE_BASE = 10000.0
CHUNK = 64
RET_HEADS, RET_DK, RET_DV = 4, 128, 256
LRU_BLOCKS, LRU_BW, CONV_W, LRU_C = 8, 128, 4, 8.0
FOX_HEADS, FOX_DH = 8, 128
D_FF = 4 * D_MODEL
LANE = 128

G_RQ, G_RK, G_RV, G_RG, G_LX, G_LY, G_FQ, G_FK, G_FV, G_GATE = 0, 4, 8, 16, 24, 32, 40, 48, 56, 64
N_GROUPS = 88
FF_COL = 2 * RET_HEADS * RET_DK + 2 * RET_HEADS * RET_DV + 2 * D_MODEL + 3 * FOX_HEADS * FOX_DH

NEG = -1e30
F32_TINY = 1.1754944e-38
LOG2E = 1.4426950408889634
VMEM_LIMIT = 56 * 1024 * 1024


def _cparams(sem):
    return pltpu.CompilerParams(dimension_semantics=sem, vmem_limit_bytes=VMEM_LIMIT)


def _sigmoid(z):
    return 1.0 / (1.0 + jnp.exp(-z))


def _softplus(z):
    return jnp.maximum(z, 0.0) + jnp.log1p(jnp.exp(-jnp.abs(z)))


def _rms(x, g):
    return x * lax.rsqrt(jnp.mean(x * x, axis=-1, keepdims=True) + EPS) * g


def _cat(ref, lo, n):
    return jnp.concatenate([ref[lo + k] for k in range(n)], axis=-1)


def _mod_kernel(ct_ref, w_ref, b_ref, o_ref, *, nb):
    ct = ct_ref[...]
    s = ct * _sigmoid(ct)
    tn = w_ref.shape[2]
    rows = []
    for b in range(nb):
        col = jnp.broadcast_to(s[:, b:b + 1], s.shape)
        parts = [jnp.sum(w_ref[0, :, n * LANE:(n + 1) * LANE] * col, axis=0, keepdims=True)
                 for n in range(tn // LANE)]
        rows.append(jnp.concatenate(parts, axis=1))
    rows.append(jnp.zeros((8 - nb, tn), F32))
    o_ref[0] = jnp.concatenate(rows, axis=0) + b_ref[0]


def _modulation(c, ada_w, ada_b):
    L, D, D6 = ada_w.shape
    B = c.shape[0]
    assert B <= 8
    c8 = jnp.zeros((D, LANE), F32).at[:, :B].set(c.T)
    tn = 1024
    out = pl.pallas_call(
        functools.partial(_mod_kernel, nb=B),
        grid=(L, D6 // tn),
        in_specs=[
            pl.BlockSpec((D, LANE), lambda l, j: (0, 0)),
            pl.BlockSpec((1, D, tn), lambda l, j: (l, 0, j)),
            pl.BlockSpec((1, 1, tn), lambda l, j: (l, 0, j)),
        ],
        out_specs=pl.BlockSpec((1, 8, tn), lambda l, j: (l, 0, j)),
        out_shape=jax.ShapeDtypeStruct((L, 8, D6), F32),
        compiler_params=_cparams(("parallel", "parallel")),
        name="adaln_mod",
    )(c8, ada_w, ada_b.reshape(L, 1, D6))
    return out[:, :B].reshape(L, B, 1, D6)


def _inproj_kernel(x_ref, mod_ref, g_ref, w_ref, wff_ref, p_ref, ff_ref, h_scr):
    D = D_MODEL

    @pl.when(pl.program_id(1) == 0)
    def _():
        h = _rms(x_ref[...], g_ref[...]) * (1.0 + mod_ref[:, D:2 * D]) + mod_ref[:, 0:D]
        hb = h.astype(BF16)
        h_scr[...] = hb
        ff_ref[...] = jnp.dot(hb, wff_ref[...], preferred_element_type=F32)

    acc = jnp.dot(h_scr[...], w_ref[...], preferred_element_type=F32)
    for g in range(acc.shape[1] // LANE):
        p_ref[g] = acc[:, g * LANE:(g + 1) * LANE].astype(BF16)


def _in_projection(x2, mod_l, g_pre, w_main, w_ff, S, tm=2048, tn=1024):
    T, D = x2.shape
    n_main = w_main.shape[1]
    nt = S // tm
    return pl.pallas_call(
        _inproj_kernel,
        grid=(T // tm, n_main // tn),
        in_specs=[
            pl.BlockSpec((tm, D), lambda i, j: (i, 0)),
            pl.BlockSpec((None, 1, 6 * D), lambda i, j: (i // nt, 0, 0)),
            pl.BlockSpec((1, D), lambda i, j: (0, 0)),
            pl.BlockSpec((D, tn), lambda i, j: (0, j)),
            pl.BlockSpec((D, LANE), lambda i, j: (0, 0)),
        ],
        out_specs=[
            pl.BlockSpec((tn // LANE, tm, LANE), lambda i, j: (j, i, 0)),
            pl.BlockSpec((tm, LANE), lambda i, j: (i, 0)),
        ],
        out_shape=[
            jax.ShapeDtypeStruct((n_main // LANE, T, LANE), BF16),
            jax.ShapeDtypeStruct((T, LANE), F32),
        ],
        scratch_shapes=[pltpu.VMEM((tm, D), BF16)],
        compiler_params=_cparams(("parallel", "arbitrary")),
        name="in_proj",
    )(x2, mod_l, g_pre.reshape(1, D), w_main, w_ff)


def _ret_kernel(q_ref, k_ref, v_ref, rg_ref, cos_ref, sin_ref, dmat_ref, qdec_ref, kdec_ref, cdec_ref,
                o_ref, state):
    @pl.when(pl.program_id(1) == 0)
    def _():
        state[...] = jnp.zeros_like(state)

    cos = cos_ref[...]
    sin = sin_ref[...]
    half = RET_DK // 2
    for h in range(RET_HEADS):
        q = q_ref[h].astype(F32)
        k = k_ref[h].astype(F32)
        qr = q * cos + pltpu.roll(q, half, 1) * sin
        kr = k * cos + pltpu.roll(k, half, 1) * sin
        v = _cat(v_ref, 2 * h, 2)
        s = lax.dot_general(qr.astype(BF16), kr.astype(BF16), (((1,), (1,)), ((), ())),
                            preferred_element_type=F32) * dmat_ref[h]
        st = state[h]
        o = jnp.dot(s.astype(BF16), v, preferred_element_type=F32)
        o = o + jnp.dot((qr * qdec_ref[h]).astype(BF16), st.astype(BF16), preferred_element_type=F32)
        kd = (kr * kdec_ref[h]).astype(BF16)
        state[h] = st * cdec_ref[h] + lax.dot_general(kd, v, (((0,), (0,)), ((), ())),
                                                      preferred_element_type=F32)
        d = o - jnp.mean(o, axis=-1, keepdims=True)
        on = d * lax.rsqrt(jnp.mean(d * d, axis=-1, keepdims=True) + EPS)
        g = _cat(rg_ref, 2 * h, 2).astype(F32)
        o_ref[:, h * RET_DV:(h + 1) * RET_DV] = (g * _sigmoid(g) * on).astype(BF16)


def _retention_tables(S, bt):
    H = RET_HEADS
    log_g = jnp.log(1.0 - 2.0 ** (-5.0 - jnp.arange(H, dtype=F32)))
    idx = jnp.arange(bt, dtype=F32)
    dist = jnp.abs(idx[:, None] - idx[None, :])
    ch = jnp.arange(bt) // CHUNK
    allowed = ch[None, :] <= ch[:, None]
    dmat = jnp.where(allowed[None], jnp.exp(dist[None] * log_g[:, None, None]), 0.0)
    qdec = jnp.exp((idx + 1.0)[None, :] * log_g[:, None])
    kdec = jnp.exp((bt - 1.0 - idx)[None, :] * log_g[:, None])
    cdec = jnp.exp(bt * log_g)
    qdec = jnp.broadcast_to(qdec[:, :, None], (H, bt, LANE))
    kdec = jnp.broadcast_to(kdec[:, :, None], (H, bt, LANE))
    cdec = jnp.broadcast_to(cdec[:, None, None], (H, 1, RET_DV))
    half = RET_DK // 2
    inv = ROPE_BASE ** (-jnp.arange(half, dtype=F32) / half)
    ang = jnp.arange(S).astype(F32)[:, None] * inv[None, :]
    cos, sin = jnp.cos(ang), jnp.sin(ang)
    cos2 = jnp.concatenate([cos, cos], axis=-1)
    sin2 = jnp.concatenate([-sin, sin], axis=-1)
    return cos2, sin2, dmat, qdec, kdec, cdec


def _retention(P, tables, B, S, bt):
    T = B * S
    nt = S // bt
    cos2, sin2, dmat, qdec, kdec, cdec = tables
    H = RET_HEADS
    tok = lambda b, i: b * nt + i
    return pl.pallas_call(
        _ret_kernel,
        grid=(B, nt),
        in_specs=[
            pl.BlockSpec((4, bt, LANE), lambda b, i: (G_RQ // 4, tok(b, i), 0)),
            pl.BlockSpec((4, bt, LANE), lambda b, i: (G_RK // 4, tok(b, i), 0)),
            pl.BlockSpec((8, bt, LANE), lambda b, i: (G_RV // 8, tok(b, i), 0)),
            pl.BlockSpec((8, bt, LANE), lambda b, i: (G_RG // 8, tok(b, i), 0)),
            pl.BlockSpec((bt, LANE), lambda b, i: (i, 0)),
            pl.BlockSpec((bt, LANE), lambda b, i: (i, 0)),
            pl.BlockSpec((H, bt, bt), lambda b, i: (0, 0, 0)),
            pl.BlockSpec((H, bt, LANE), lambda b, i: (0, 0, 0)),
            pl.BlockSpec((H, bt, LANE), lambda b, i: (0, 0, 0)),
            pl.BlockSpec((H, 1, RET_DV), lambda b, i: (0, 0, 0)),
        ],
        out_specs=pl.BlockSpec((bt, H * RET_DV), lambda b, i: (tok(b, i), 0)),
        out_shape=jax.ShapeDtypeStruct((T, H * RET_DV), BF16),
        scratch_shapes=[pltpu.VMEM((H, RET_DK, RET_DV), F32)],
        compiler_params=_cparams(("parallel", "arbitrary")),
        name="retention",
    )(P, P, P, P, cos2, sin2, dmat, qdec, kdec, cdec)


def _lru_kernel(lx_ref, ly_ref, cw_ref, cb_ref, wax_ref, ba_ref, bx_ref, lam_ref, o_ref,
                tail, hprev, a_scr, b_scr, h_scr):
    bt = lx_ref.shape[1]

    @pl.when(pl.program_id(1) == 0)
    def _():
        tail[...] = jnp.zeros_like(tail)
        hprev[...] = jnp.zeros_like(hprev)

    r8 = lax.broadcasted_iota(jnp.int32, (bt // 8, 8, LANE), 1)
    for n in range(LRU_BLOCKS):
        sl = slice(n * LANE, (n + 1) * LANE)
        x = lx_ref[n].astype(F32)
        xp = jnp.concatenate([tail[:, sl], x], axis=0)
        u = cb_ref[:, sl] + x * cw_ref[CONV_W - 1:CONV_W, sl]
        for k in range(1, CONV_W):
            u = u + pltpu.roll(xp, k, 0)[8:] * cw_ref[CONV_W - 1 - k:CONV_W - k, sl]
        tail[:, sl] = x[bt - 8:bt]
        gates = jnp.dot(u.astype(BF16), wax_ref[n], preferred_element_type=F32)
        r = 1.0 / (1.0 + jnp.exp2(gates[:, :LANE] + ba_ref[:, sl]))
        ig = 1.0 / (1.0 + jnp.exp2(gates[:, LANE:] + bx_ref[:, sl]))
        a = jnp.exp2(r * ((-LRU_C * LOG2E) * _softplus(-lam_ref[:, sl])))
        v = jnp.maximum(1.0 - a * a, 0.0)
        bb = v * lax.rsqrt(jnp.maximum(v, F32_TINY)) * (ig * u)
        a = a.reshape(bt // 8, 8, LANE)
        bb = bb.reshape(bt // 8, 8, LANE)
        for d in (1, 2, 4):
            m = r8 >= d
            a_sh = jnp.where(m, pltpu.roll(a, d, 1), 1.0)
            b_sh = jnp.where(m, pltpu.roll(bb, d, 1), 0.0)
            bb = a * b_sh + bb
            a = a * a_sh
        a_scr[:, sl] = a.reshape(bt, LANE)
        b_scr[:, sl] = bb.reshape(bt, LANE)

    def body(c, hp):
        r0 = pl.multiple_of(c * 8, 8)
        h = a_scr[pl.ds(r0, 8), :] * hp + b_scr[pl.ds(r0, 8), :]
        h_scr[pl.ds(r0, 8), :] = h
        return jnp.broadcast_to(h[7:8, :], h.shape)

    hprev[...] = lax.fori_loop(0, bt // 8, body, hprev[...], unroll=4)
    for n in range(LRU_BLOCKS):
        sl = slice(n * LANE, (n + 1) * LANE)
        o_ref[:, sl] = (jax.nn.gelu(ly_ref[n].astype(F32)) * h_scr[:, sl]).astype(BF16)


def _rg_lru(P, conv_w, conv_b, w_ax, b_a, b_x, lam, B, S, bt):
    T = B * S
    nt = S // bt
    W = LRU_BLOCKS * LRU_BW
    tok = lambda b, i: b * nt + i
    full = lambda shape: pl.BlockSpec(shape, lambda b, i: (0,) * len(shape))
    return pl.pallas_call(
        _lru_kernel,
        grid=(B, nt),
        in_specs=[
            pl.BlockSpec((8, bt, LANE), lambda b, i: (G_LX // 8, tok(b, i), 0)),
            pl.BlockSpec((8, bt, LANE), lambda b, i: (G_LY // 8, tok(b, i), 0)),
            full((CONV_W, W)), full((1, W)), full((LRU_BLOCKS, LRU_BW, 2 * LRU_BW)),
            full((1, W)), full((1, W)), full((1, W)),
        ],
        out_specs=pl.BlockSpec((bt, W), lambda b, i: (tok(b, i), 0)),
        out_shape=jax.ShapeDtypeStruct((T, W), BF16),
        scratch_shapes=[pltpu.VMEM((8, W), F32), pltpu.VMEM((8, W), F32),
                        pltpu.VMEM((bt, W), F32), pltpu.VMEM((bt, W), F32), pltpu.VMEM((bt, W), F32)],
        compiler_params=_cparams(("parallel", "arbitrary")),
        name="rg_lru",
    )(P, P, conv_w, conv_b.reshape(1, W), w_ax, b_a.reshape(1, W), b_x.reshape(1, W), lam.reshape(1, W))


def _cum_kernel(ff_ref, bf_ref, tri_ref, route_ref, const_ref, aq_ref, ak_ref, cl_ref, carry):
    bc = ff_ref.shape[0]

    @pl.when(pl.program_id(1) == 0)
    def _():
        carry[...] = jnp.zeros_like(carry)

    z = ff_ref[...] + bf_ref[...]
    logf = jnp.minimum(z, 0.0) - jnp.log1p(jnp.exp(-jnp.abs(z)))
    l1 = logf.astype(BF16)
    e1 = logf - l1.astype(F32)
    l2 = e1.astype(BF16)
    l3 = (e1 - l2.astype(F32)).astype(BF16)
    cs = jnp.dot(tri_ref[...], jnp.concatenate([l1, l2, l3], axis=1), preferred_element_type=F32)
    cum = (cs[:, :LANE] + cs[:, LANE:2 * LANE]) + cs[:, 2 * LANE:] + carry[...]
    carry[...] = cum[bc - 1:bc]
    cum = cum * LOG2E
    cl_ref[0] = cum[bc - 1:bc]
    hi = cum.astype(BF16).astype(F32)
    r1 = cum - hi
    mid = r1.astype(BF16).astype(F32)
    lo = r1 - mid
    lane = lax.broadcasted_iota(jnp.int32, (bc, LANE), 1)
    H = FOX_HEADS
    pieces = jnp.where(lane < H, hi, jnp.where(lane < 2 * H, pltpu.roll(mid, H, 1), pltpu.roll(lo, 2 * H, 1)))
    out = jnp.dot(pieces.astype(BF16), route_ref[...], preferred_element_type=F32) + const_ref[...]
    for h in range(H):
        aq_ref[h] = out[:, h * LANE:(h + 1) * LANE].astype(BF16)
        ak_ref[h] = out[:, (H + h) * LANE:(H + h + 1) * LANE].astype(BF16)


def _forget_cumsum(ff, b_f, B, S, tq, bc=256):
    T = B * S
    nt = S // bc
    bf = jnp.zeros((1, LANE), F32).at[0, :FOX_HEADS].set(b_f)
    tri = (jnp.arange(bc)[None, :] <= jnp.arange(bc)[:, None]).astype(BF16)
    H = FOX_HEADS
    hh, pp = jnp.meshgrid(jnp.arange(H), jnp.arange(3), indexing="ij")
    route = jnp.zeros((LANE, 2 * H, LANE), F32)
    route = route.at[pp * H + hh, hh, pp].set(1.0).at[pp * H + hh, H + hh, 3 + pp].set(-1.0)
    route = route.reshape(LANE, 2 * H * LANE).astype(BF16)
    const = jnp.zeros((2 * H, LANE), F32).at[:H, 3:6].set(1.0).at[H:, 0:3].set(1.0).reshape(1, 2 * H * LANE)
    tok = lambda b, i: b * nt + i
    out = jax.ShapeDtypeStruct((FOX_HEADS, T, LANE), BF16)
    aq, ak, cl = pl.pallas_call(
        _cum_kernel,
        grid=(B, nt),
        in_specs=[
            pl.BlockSpec((bc, LANE), lambda b, i: (tok(b, i), 0)),
            pl.BlockSpec((1, LANE), lambda b, i: (0, 0)),
            pl.BlockSpec((bc, bc), lambda b, i: (0, 0)),
            pl.BlockSpec((LANE, 2 * H * LANE), lambda b, i: (0, 0)),
            pl.BlockSpec((1, 2 * H * LANE), lambda b, i: (0, 0)),
        ],
        out_specs=[pl.BlockSpec((FOX_HEADS, bc, LANE), lambda b, i: (0, tok(b, i), 0))] * 2
        + [pl.BlockSpec((1, 1, LANE), lambda b, i: (tok(b, i), 0, 0))],
        out_shape=[out, out, jax.ShapeDtypeStruct((B * nt, 1, LANE), F32)],
        scratch_shapes=[pltpu.VMEM((1, LANE), F32)],
        compiler_params=_cparams(("parallel", "arbitrary")),
        name="forget_cumsum",
    )(ff, bf, tri, route, const)
    per = tq // bc
    nq = S // tq
    last = cl.reshape(B, nt, LANE)[:, per - 1::per, :FOX_HEADS]
    ncl = jnp.zeros((B, FOX_HEADS, LANE), F32).at[:, :, :nq].set(-last.transpose(0, 2, 1))
    return aq, ak, ncl.reshape(B * FOX_HEADS, 1, LANE)


FOX_UNROLL = 4
FOX_SUBSTEPS = 2
SUB = 256
BOUND_SLACK = 1.02
SKIP_MARGIN = 130.0


def _token_sums(w, x):
    return lax.dot_general(w, x, (((1,), (1,)), ((), ())), preferred_element_type=F32)


def _fox_block(i, n_blocks, q, aq, qn, aqn, out, ncl_ref, k_ref, ak_ref, v_ref, m_scr, acc_scr, kmax_scr,
               n_smem):
    tq = q.shape[0]
    token_sums = _token_sums
    ones_row = jnp.ones((8, LANE), BF16)
    qa = jnp.concatenate([q, aq], axis=-1)
    ones = jnp.ones((SUB, LANE), BF16)
    nsub = tq // SUB

    def scores(j):
        out = []
        for h in range(nsub):
            r0 = pl.multiple_of(j * tq + h * SUB, SUB)
            ka = jnp.concatenate([k_ref[0, pl.ds(r0, SUB), :], ak_ref[0, pl.ds(r0, SUB), :]], axis=-1)
            out.append(lax.dot_general(qa, ka, (((1,), (1,)), ((), ())), preferred_element_type=F32))
        return out

    def update(s, j, m_prev, acc):
        mx = jnp.max(s[0], axis=1, keepdims=True)
        for sh in s[1:]:
            mx = jnp.maximum(mx, jnp.max(sh, axis=1, keepdims=True))
        m_new = jnp.broadcast_to(mx, m_scr.shape) if m_prev is None else jnp.maximum(m_prev, mx)
        m_rep = jnp.concatenate([m_new] * (SUB // LANE), axis=1)
        pv = None
        for h, sh in enumerate(s):
            r0 = pl.multiple_of(j * tq + h * SUB, SUB)
            va = jnp.concatenate([v_ref[0, pl.ds(r0, SUB), :], ones], axis=-1)
            d = jnp.dot(jnp.exp2(sh - m_rep).astype(BF16), va, preferred_element_type=F32)
            pv = d if pv is None else pv + d
        if m_prev is not None:
            alpha = jnp.exp2(m_prev - m_new)
            pv = acc * jnp.concatenate([alpha, alpha], axis=-1) + pv
        return m_new, pv

    row = lax.broadcasted_iota(jnp.int32, (tq, SUB), 0)
    col = lax.broadcasted_iota(jnp.int32, (tq, SUB), 1)

    def run(js, first=False):
        s_list = [scores(j) for j in js]
        if first:
            s_list[0] = [jnp.where(col + h * SUB <= row, sh, NEG) for h, sh in enumerate(s_list[0])]
            m, acc = None, None
        else:
            m, acc = m_scr[...], acc_scr[...]
        for s, j in zip(s_list, js):
            m, acc = update(s, j, m, acc)
        m_scr[...] = m
        acc_scr[...] = acc

    def next_walk_length():
        i1 = jnp.minimum(i + 1, n_blocks - 1)
        kd = k_ref[0, pl.ds(pl.multiple_of(i1 * tq, tq), tq), :]
        wcol = lax.broadcasted_iota(jnp.int32, (8, 2 * LANE), 1)
        w_mc = jnp.where(wcol < LANE, 1.0, jnp.where(wcol < LANE + 3, -1.0, 0.0)).astype(BF16)
        mc = token_sums(w_mc, jnp.concatenate([qn * kd, aqn], axis=-1))
        qnorm = jnp.sqrt(token_sums(ones_row, qn * qn))
        r = mc - SKIP_MARGIN - BOUND_SLACK * qnorm * kmax_scr[:, 0:1]
        r_min = jnp.min(r, axis=1, keepdims=True)[0:1]
        lane = lax.broadcasted_iota(jnp.int32, (1, LANE), 1)
        keep = jnp.logical_and(lane < i1, ncl_ref[...] >= r_min)
        n_smem[0] = jnp.sum(jnp.where(keep, 1.0, 0.0)).astype(jnp.int32)

    n = jnp.where(i == 0, 0, n_smem[0])

    lead = jnp.minimum(n, FOX_UNROLL - 1)
    for size in range(FOX_UNROLL):

        @pl.when(lead == size)
        def _(size=size):
            next_walk_length()
            run([i - u for u in range(size + 1)], first=True)

    def group(t, carry):
        j0 = i - FOX_UNROLL - t * FOX_UNROLL
        run([j0 - u for u in range(FOX_UNROLL)])
        return carry

    ng = (n - lead) // FOX_UNROLL
    lax.fori_loop(0, ng, group, 0)
    done = lead + ng * FOX_UNROLL
    size = FOX_UNROLL // 2
    while size >= 1:
        take = ((n - done) & size) != 0
        j0 = i - 1 - done

        @pl.when(take)
        def _(j0=j0, size=size):
            run([j0 - u for u in range(size)])

        done = done + jnp.where(take, size, 0)
        size //= 2
    acc = acc_scr[...]
    out[...] = (acc[:, :LANE] / acc[:, LANE:]).astype(BF16)


def _fox_kernel(ncl_ref, q_ref, aq_ref, qx_ref, aqx_ref, k_ref, ak_ref, v_ref, o_ref,
                m_scr, acc_scr, kmax_scr, n_smem):
    tq = q_ref.shape[1] // FOX_SUBSTEPS
    S = k_ref.shape[1]
    step = pl.program_id(1)

    @pl.when(step == 0)
    def _():
        ones_row = jnp.ones((8, LANE), BF16)

        def body(c, mx):
            r0 = pl.multiple_of(c * tq, tq)
            kc = k_ref[0, pl.ds(r0, tq), :]
            return jnp.maximum(mx, _token_sums(ones_row, kc * kc))

        mx = lax.fori_loop(0, S // tq, body, jnp.zeros((8, tq), F32))
        kmax_scr[...] = jnp.broadcast_to(jnp.sqrt(jnp.max(mx, axis=1, keepdims=True)), kmax_scr.shape)

    for sb in range(FOX_SUBSTEPS):
        rows = slice(sb * tq, (sb + 1) * tq)
        if sb + 1 < FOX_SUBSTEPS:
            nrows = slice((sb + 1) * tq, (sb + 2) * tq)
            qn, aqn = q_ref[0, nrows, :], aq_ref[0, nrows, :]
        else:
            qn, aqn = qx_ref[0], aqx_ref[0]
        _fox_block(step * FOX_SUBSTEPS + sb, pl.num_programs(1) * FOX_SUBSTEPS, q_ref[0, rows, :],
                   aq_ref[0, rows, :], qn, aqn, o_ref.at[rows, :], ncl_ref, k_ref, ak_ref, v_ref,
                   m_scr, acc_scr, kmax_scr, n_smem)


def _fox_attention(P, Aq, Ak, ncl, B, S, tq):
    T = B * S
    nq = S // tq
    ns = nq // FOX_SUBSTEPS
    ts = tq * FOX_SUBSTEPS
    H = FOX_HEADS
    assert nq <= LANE and nq % FOX_SUBSTEPS == 0
    nxt = lambda i: jnp.minimum((i + 1) * FOX_SUBSTEPS, nq - 1)
    return pl.pallas_call(
        _fox_kernel,
        grid=(B * H, ns),
        in_specs=[
            pl.BlockSpec((None, 1, LANE), lambda g, i: (g, 0, 0)),
            pl.BlockSpec((1, ts, LANE), lambda g, i: (G_FQ + g % H, (g // H) * ns + i, 0)),
            pl.BlockSpec((1, ts, LANE), lambda g, i: (g % H, (g // H) * ns + i, 0)),
            pl.BlockSpec((1, tq, LANE), lambda g, i: (G_FQ + g % H, (g // H) * nq + nxt(i), 0)),
            pl.BlockSpec((1, tq, LANE), lambda g, i: (g % H, (g // H) * nq + nxt(i), 0)),
            pl.BlockSpec((1, S, LANE), lambda g, i: (G_FK + g % H, g // H, 0)),
            pl.BlockSpec((1, S, LANE), lambda g, i: (g % H, g // H, 0)),
            pl.BlockSpec((1, S, LANE), lambda g, i: (G_FV + g % H, g // H, 0)),
        ],
        out_specs=pl.BlockSpec((ts, LANE), lambda g, i: ((g // H) * ns + i, g % H)),
        out_shape=jax.ShapeDtypeStruct((T, H * FOX_DH), BF16),
        scratch_shapes=[pltpu.VMEM((tq, LANE), F32), pltpu.VMEM((tq, 2 * LANE), F32),
                        pltpu.VMEM((8, LANE), F32), pltpu.SMEM((1,), jnp.int32)],
        compiler_params=_cparams(("parallel", "arbitrary")),
        name="fox_attention",
    )(ncl, P, Aq, P, Aq, P, Ak, P)


def _merge_kernel(yr_ref, yl_ref, yf_ref, g0_ref, g1_ref, g2_ref, x_ref, mod_ref, gp_ref,
                  wr_ref, wl_ref, wf_ref, wo_ref, o_ref):
    D = D_MODEL

    def branch(y_ref, w_ref, g_ref):
        g = _sigmoid(_cat(g_ref, 0, 8).astype(F32))
        return g * jnp.dot(y_ref[...], w_ref[...], preferred_element_type=F32)

    m = branch(yr_ref, wr_ref, g0_ref) + branch(yl_ref, wl_ref, g1_ref) + branch(yf_ref, wf_ref, g2_ref)
    z = jnp.dot(m.astype(BF16), wo_ref[...], preferred_element_type=F32)
    o_ref[...] = x_ref[...] + mod_ref[:, 2 * D:3 * D] * _rms(z, gp_ref[...])


def _merge(y_ret, y_lru, y_fox, P, x2, mod_l, g_post, w_r, w_l, w_f, w_o, S, tm=512):
    T, D = x2.shape
    nt = S // tm
    row = pl.BlockSpec((tm, D), lambda i: (i, 0))
    wspec = pl.BlockSpec((D, D), lambda i: (0, 0))
    gate = lambda k: pl.BlockSpec((8, tm, LANE), lambda i: (G_GATE // 8 + k, i, 0))
    return pl.pallas_call(
        _merge_kernel,
        grid=(T // tm,),
        in_specs=[row, row, row, gate(0), gate(1), gate(2), row,
                  pl.BlockSpec((None, 1, 6 * D), lambda i: (i // nt, 0, 0)),
                  pl.BlockSpec((1, D), lambda i: (0, 0)),
                  wspec, wspec, wspec, wspec],
        out_specs=row,
        out_shape=jax.ShapeDtypeStruct((T, D), F32),
        compiler_params=_cparams(("parallel",)),
        name="merge_out_proj",
    )(y_ret, y_lru, y_fox, P, P, P, x2, mod_l, g_post.reshape(1, D), w_r, w_l, w_f, w_o)


def _mlp_kernel(x_ref, mod_ref, gpre_ref, gpost_ref, w1_ref, w2_ref, o_ref):
    D = D_MODEL
    x = x_ref[...]
    h = (_rms(x, gpre_ref[...]) * (1.0 + mod_ref[:, 4 * D:5 * D]) + mod_ref[:, 3 * D:4 * D]).astype(BF16)
    f = jnp.zeros(x.shape, F32)
    ck = 1024
    for c in range(D_FF // ck):
        a = jnp.maximum(jnp.dot(h, w1_ref[:, c * ck:(c + 1) * ck], preferred_element_type=F32), 0.0)
        f = f + jnp.dot((a * a).astype(BF16), w2_ref[c * ck:(c + 1) * ck, :], preferred_element_type=F32)
    o_ref[...] = x + mod_ref[:, 5 * D:6 * D] * _rms(f, gpost_ref[...])


def _mlp(x2, mod_l, g_pre, g_post, w1, w2, S, tm=512):
    T, D = x2.shape
    nt = S // tm
    row = pl.BlockSpec((tm, D), lambda i: (i, 0))
    vec = pl.BlockSpec((1, D), lambda i: (0, 0))
    return pl.pallas_call(
        _mlp_kernel,
        grid=(T // tm,),
        in_specs=[row, pl.BlockSpec((None, 1, 6 * D), lambda i: (i // nt, 0, 0)), vec, vec,
                  pl.BlockSpec((D, D_FF), lambda i: (0, 0)), pl.BlockSpec((D_FF, D), lambda i: (0, 0))],
        out_specs=row,
        out_shape=jax.ShapeDtypeStruct((T, D), F32),
        compiler_params=_cparams(("parallel",)),
        name="relu2_mlp",
    )(x2, mod_l, g_pre.reshape(1, D), g_post.reshape(1, D), w1, w2)


def _prep_w_in(w_in):
    rk0, rk1 = G_RK * LANE, G_RV * LANE
    fq0, fq1 = G_FQ * LANE, G_FK * LANE
    parts = [w_in[:, :rk0], w_in[:, rk0:rk1] * RET_DK ** -0.5, w_in[:, rk1:fq0],
             w_in[:, fq0:fq1] * (FOX_DH ** -0.5 * LOG2E), w_in[:, fq1:FF_COL], w_in[:, FF_COL + FOX_HEADS:]]
    w_main = jnp.concatenate([p.astype(BF16) for p in parts], axis=1)
    w_ff = jnp.zeros((w_in.shape[0], LANE), BF16).at[:, :FOX_HEADS].set(
        w_in[:, FF_COL:FF_COL + FOX_HEADS].astype(BF16))
    return w_main, w_ff


def kernel(x, c, ada_w, ada_b, norm_pre_mix, norm_post_mix, norm_pre_mlp, norm_post_mlp, w_in, conv_w, conv_b,
           lru_w_a, lru_b_a, lru_w_x, lru_b_x, lru_lambda, fox_b_f, ret_w_o, lru_w_o, fox_w_o, w_out, mlp_w1,
           mlp_w2):
    B, S, D = x.shape
    L = ada_w.shape[0]
    assert D == D_MODEL and S % 2048 == 0
    ret_bt, lru_bt, fox_tq = 256, 512, 512
    mod = _modulation(c, ada_w, ada_b)
    tables = _retention_tables(S, ret_bt)
    x2 = x.reshape(B * S, D)
    for l in range(L):
        w_main, w_ff = _prep_w_in(w_in[l])
        P, ff = _in_projection(x2, mod[l], norm_pre_mix[l], w_main, w_ff, S)
        y_ret = _retention(P, tables, B, S, ret_bt)
        w_ax = (jnp.concatenate([lru_w_a[l], lru_w_x[l]], axis=-1) * -LOG2E).astype(BF16)
        y_lru = _rg_lru(P, conv_w[l], conv_b[l], w_ax, lru_b_a[l] * -LOG2E, lru_b_x[l] * -LOG2E,
                        lru_lambda[l], B, S, lru_bt)
        Aq, Ak, ncl = _forget_cumsum(ff, fox_b_f[l], B, S, fox_tq)
        y_fox = _fox_attention(P, Aq, Ak, ncl, B, S, fox_tq)
        x2 = _merge(y_ret, y_lru, y_fox, P, x2, mod[l], norm_post_mix[l], ret_w_o[l].astype(BF16),
                    lru_w_o[l].astype(BF16), fox_w_o[l].astype(BF16), w_out[l].astype(BF16), S)
        x2 = _mlp(x2, mod[l], norm_pre_mlp[l], norm_post_mlp[l], mlp_w1[l].astype(BF16),
                  mlp_w2[l].astype(BF16), S)
    return x2.reshape(B, S, D)
```

```python
import functools

import jax
import jax.numpy as jnp
from jax import lax
from jax.experimental import pallas as pl
from jax.experimental.pallas import tpu as pltpu

F32 = jnp.float32
BF16 = jnp.bfloat16

D_MODEL = 1024
EPS = 1e-6
ROPE_BASE = 10000.0
CHUNK = 64
RET_HEADS, RET_DK, RET_DV = 4, 128, 256
LRU_BLOCKS, LRU_BW, CONV_W, LRU_C = 8, 128, 4, 8.0
FOX_HEADS, FOX_DH = 8, 128
D_FF = 4 * D_MODEL
LANE = 128

G_RQ, G_RK, G_RV, G_RG, G_LX, G_LY, G_FQ, G_FK, G_FV, G_GATE = 0, 4, 8, 16, 24, 32, 40, 48, 56, 64
N_GROUPS = 88
FF_COL = 2 * RET_HEADS * RET_DK + 2 * RET_HEADS * RET_DV + 2 * D_MODEL + 3 * FOX_HEADS * FOX_DH

NEG = -1e30
F32_TINY = 1.1754944e-38
LOG2E = 1.4426950408889634
VMEM_LIMIT = 56 * 1024 * 1024


def _cparams(sem):
    return pltpu.CompilerParams(dimension_semantics=sem, vmem_limit_bytes=VMEM_LIMIT)


def _sigmoid(z):
    return 1.0 / (1.0 + jnp.exp(-z))


def _softplus(z):
    return jnp.maximum(z, 0.0) + jnp.log1p(jnp.exp(-jnp.abs(z)))


def _rms(x, g):
    return x * lax.rsqrt(jnp.mean(x * x, axis=-1, keepdims=True) + EPS) * g


def _cat(ref, lo, n):
    return jnp.concatenate([ref[lo + k] for k in range(n)], axis=-1)


def _mod_kernel(ct_ref, w_ref, b_ref, o_ref, *, nb):
    ct = ct_ref[...]
    s = ct * _sigmoid(ct)
    tn = w_ref.shape[2]
    rows = []
    for b in range(nb):
        col = jnp.broadcast_to(s[:, b:b + 1], s.shape)
        parts = [jnp.sum(w_ref[0, :, n * LANE:(n + 1) * LANE] * col, axis=0, keepdims=True)
                 for n in range(tn // LANE)]
        rows.append(jnp.concatenate(parts, axis=1))
    rows.append(jnp.zeros((8 - nb, tn), F32))
    o_ref[0] = jnp.concatenate(rows, axis=0) + b_ref[0]


def _modulation(c, ada_w, ada_b):
    L, D, D6 = ada_w.shape
    B = c.shape[0]
    assert B <= 8
    c8 = jnp.zeros((D, LANE), F32).at[:, :B].set(c.T)
    tn = 1024
    out = pl.pallas_call(
        functools.partial(_mod_kernel, nb=B),
        grid=(L, D6 // tn),
        in_specs=[
            pl.BlockSpec((D, LANE), lambda l, j: (0, 0)),
            pl.BlockSpec((1, D, tn), lambda l, j: (l, 0, j)),
            pl.BlockSpec((1, 1, tn), lambda l, j: (l, 0, j)),
        ],
        out_specs=pl.BlockSpec((1, 8, tn), lambda l, j: (l, 0, j)),
        out_shape=jax.ShapeDtypeStruct((L, 8, D6), F32),
        compiler_params=_cparams(("parallel", "parallel")),
        name="adaln_mod",
    )(c8, ada_w, ada_b.reshape(L, 1, D6))
    return out[:, :B].reshape(L, B, 1, D6)


def _inproj_kernel(x_ref, mod_ref, g_ref, w_ref, wff_ref, p_ref, ff_ref, h_scr):
    D = D_MODEL

    @pl.when(pl.program_id(1) == 0)
    def _():
        h = _rms(x_ref[...], g_ref[...]) * (1.0 + mod_ref[:, D:2 * D]) + mod_ref[:, 0:D]
        hb = h.astype(BF16)
        h_scr[...] = hb
        ff_ref[...] = jnp.dot(hb, wff_ref[...], preferred_element_type=F32)

    acc = jnp.dot(h_scr[...], w_ref[...], preferred_element_type=F32)
    for g in range(acc.shape[1] // LANE):
        p_ref[g] = acc[:, g * LANE:(g + 1) * LANE].astype(BF16)


def _in_projection(x2, mod_l, g_pre, w_main, w_ff, layer, S, tm=2048, tn=1024):
    T, D = x2.shape
    n_main = w_main.shape[2]
    nt = S // tm
    return pl.pallas_call(
        _inproj_kernel,
        grid=(T // tm, n_main // tn),
        in_specs=[
            pl.BlockSpec((tm, D), lambda i, j: (i, 0)),
            pl.BlockSpec((None, 1, 6 * D), lambda i, j: (i // nt, 0, 0)),
            pl.BlockSpec((1, D), lambda i, j: (0, 0)),
            pl.BlockSpec((None, D, tn), lambda i, j: (layer, 0, j)),
            pl.BlockSpec((None, D, LANE), lambda i, j: (layer, 0, 0)),
        ],
        out_specs=[
            pl.BlockSpec((tn // LANE, tm, LANE), lambda i, j: (j, i, 0)),
            pl.BlockSpec((tm, LANE), lambda i, j: (i, 0)),
        ],
        out_shape=[
            jax.ShapeDtypeStruct((n_main // LANE, T, LANE), BF16),
            jax.ShapeDtypeStruct((T, LANE), F32),
        ],
        scratch_shapes=[pltpu.VMEM((tm, D), BF16)],
        compiler_params=_cparams(("parallel", "arbitrary")),
        name="in_proj",
    )(x2, mod_l, g_pre.reshape(1, D), w_main, w_ff)


def _ret_kernel(q_ref, k_ref, v_ref, rg_ref, cos_ref, sin_ref, dmat_ref, qdec_ref, kdec_ref, cdec_ref,
                o_ref, state):
    @pl.when(pl.program_id(1) == 0)
    def _():
        state[...] = jnp.zeros_like(state)

    cos = cos_ref[...]
    sin = sin_ref[...]
    half = RET_DK // 2
    for h in range(RET_HEADS):
        q = q_ref[h].astype(F32)
        k = k_ref[h].astype(F32)
        qr = q * cos + pltpu.roll(q, half, 1) * sin
        kr = k * cos + pltpu.roll(k, half, 1) * sin
        v = _cat(v_ref, 2 * h, 2)
        s = lax.dot_general(qr.astype(BF16), kr.astype(BF16), (((1,), (1,)), ((), ())),
                            preferred_element_type=F32) * dmat_ref[h]
        st = state[h]
        o = jnp.dot(s.astype(BF16), v, preferred_element_type=F32)
        o = o + jnp.dot((qr * qdec_ref[h]).astype(BF16), st.astype(BF16), preferred_element_type=F32)
        kd = (kr * kdec_ref[h]).astype(BF16)
        state[h] = st * cdec_ref[h] + lax.dot_general(kd, v, (((0,), (0,)), ((), ())),
                                                      preferred_element_type=F32)
        d = o - jnp.mean(o, axis=-1, keepdims=True)
        on = d * lax.rsqrt(jnp.mean(d * d, axis=-1, keepdims=True) + EPS)
        g = _cat(rg_ref, 2 * h, 2).astype(F32)
        o_ref[:, h * RET_DV:(h + 1) * RET_DV] = (g * _sigmoid(g) * on).astype(BF16)


def _retention_tables(S, bt):
    H = RET_HEADS
    log_g = jnp.log(1.0 - 2.0 ** (-5.0 - jnp.arange(H, dtype=F32)))
    idx = jnp.arange(bt, dtype=F32)
    dist = jnp.abs(idx[:, None] - idx[None, :])
    ch = jnp.arange(bt) // CHUNK
    allowed = ch[None, :] <= ch[:, None]
    dmat = jnp.where(allowed[None], jnp.exp(dist[None] * log_g[:, None, None]), 0.0)
    qdec = jnp.exp((idx + 1.0)[None, :] * log_g[:, None])
    kdec = jnp.exp((bt - 1.0 - idx)[None, :] * log_g[:, None])
    cdec = jnp.exp(bt * log_g)
    qdec = jnp.broadcast_to(qdec[:, :, None], (H, bt, LANE))
    kdec = jnp.broadcast_to(kdec[:, :, None], (H, bt, LANE))
    cdec = jnp.broadcast_to(cdec[:, None, None], (H, 1, RET_DV))
    half = RET_DK // 2
    inv = ROPE_BASE ** (-jnp.arange(half, dtype=F32) / half)
    a_hi = (jnp.arange(S // LANE) * LANE).astype(F32)[:, None] * inv[None, :]
    a_lo = jnp.arange(LANE).astype(F32)[:, None] * inv[None, :]
    ch, sh = jnp.cos(a_hi)[:, None, :], jnp.sin(a_hi)[:, None, :]
    cl, sl = jnp.cos(a_lo)[None, :, :], jnp.sin(a_lo)[None, :, :]
    cos = (ch * cl - sh * sl).reshape(S, half)
    sin = (sh * cl + ch * sl).reshape(S, half)
    cos2 = jnp.concatenate([cos, cos], axis=-1)
    sin2 = jnp.concatenate([-sin, sin], axis=-1)
    return cos2, sin2, dmat, qdec, kdec, cdec


def _retention(P, tables, B, S, bt):
    T = B * S
    nt = S // bt
    cos2, sin2, dmat, qdec, kdec, cdec = tables
    H = RET_HEADS
    tok = lambda b, i: b * nt + i
    return pl.pallas_call(
        _ret_kernel,
        grid=(B, nt),
        in_specs=[
            pl.BlockSpec((4, bt, LANE), lambda b, i: (G_RQ // 4, tok(b, i), 0)),
            pl.BlockSpec((4, bt, LANE), lambda b, i: (G_RK // 4, tok(b, i), 0)),
            pl.BlockSpec((8, bt, LANE), lambda b, i: (G_RV // 8, tok(b, i), 0)),
            pl.BlockSpec((8, bt, LANE), lambda b, i: (G_RG // 8, tok(b, i), 0)),
            pl.BlockSpec((bt, LANE), lambda b, i: (i, 0)),
            pl.BlockSpec((bt, LANE), lambda b, i: (i, 0)),
            pl.BlockSpec((H, bt, bt), lambda b, i: (0, 0, 0)),
            pl.BlockSpec((H, bt, LANE), lambda b, i: (0, 0, 0)),
            pl.BlockSpec((H, bt, LANE), lambda b, i: (0, 0, 0)),
            pl.BlockSpec((H, 1, RET_DV), lambda b, i: (0, 0, 0)),
        ],
        out_specs=pl.BlockSpec((bt, H * RET_DV), lambda b, i: (tok(b, i), 0)),
        out_shape=jax.ShapeDtypeStruct((T, H * RET_DV), BF16),
        scratch_shapes=[pltpu.VMEM((H, RET_DK, RET_DV), F32)],
        compiler_params=_cparams(("parallel", "arbitrary")),
        name="retention",
    )(P, P, P, P, cos2, sin2, dmat, qdec, kdec, cdec)


def _lru_kernel(lx_ref, ly_ref, cw_ref, cb_ref, wax_ref, ba_ref, bx_ref, lam_ref, o_ref,
                tail, hprev, a_scr, b_scr, h_scr):
    bt = lx_ref.shape[1]

    @pl.when(pl.program_id(1) == 0)
    def _():
        tail[...] = jnp.zeros_like(tail)
        hprev[...] = jnp.zeros_like(hprev)

    r8 = lax.broadcasted_iota(jnp.int32, (bt // 8, 8, LANE), 1)
    for n in range(LRU_BLOCKS):
        sl = slice(n * LANE, (n + 1) * LANE)
        x = lx_ref[n].astype(F32)
        xp = jnp.concatenate([tail[:, sl], x], axis=0)
        u = cb_ref[:, sl] + x * cw_ref[CONV_W - 1:CONV_W, sl]
        for k in range(1, CONV_W):
            u = u + pltpu.roll(xp, k, 0)[8:] * cw_ref[CONV_W - 1 - k:CONV_W - k, sl]
        tail[:, sl] = x[bt - 8:bt]
        gates = jnp.dot(u.astype(BF16), wax_ref[n], preferred_element_type=F32)
        r = 1.0 / (1.0 + jnp.exp2(gates[:, :LANE] + ba_ref[:, sl]))
        ig = 1.0 / (1.0 + jnp.exp2(gates[:, LANE:] + bx_ref[:, sl]))
        a = jnp.exp2(r * ((-LRU_C * LOG2E) * _softplus(-lam_ref[:, sl])))
        v = jnp.maximum(1.0 - a * a, 0.0)
        bb = v * lax.rsqrt(jnp.maximum(v, F32_TINY)) * (ig * u)
        a = a.reshape(bt // 8, 8, LANE)
        bb = bb.reshape(bt // 8, 8, LANE)
        for d in (1, 2, 4):
            m = r8 >= d
            a_sh = jnp.where(m, pltpu.roll(a, d, 1), 1.0)
            b_sh = jnp.where(m, pltpu.roll(bb, d, 1), 0.0)
            bb = a * b_sh + bb
            a = a * a_sh
        a_scr[:, sl] = a.reshape(bt, LANE)
        b_scr[:, sl] = bb.reshape(bt, LANE)

    def body(c, hp):
        r0 = pl.multiple_of(c * 8, 8)
        h = a_scr[pl.ds(r0, 8), :] * hp + b_scr[pl.ds(r0, 8), :]
        h_scr[pl.ds(r0, 8), :] = h
        return jnp.broadcast_to(h[7:8, :], h.shape)

    hprev[...] = lax.fori_loop(0, bt // 8, body, hprev[...], unroll=4)
    for n in range(LRU_BLOCKS):
        sl = slice(n * LANE, (n + 1) * LANE)
        o_ref[:, sl] = (jax.nn.gelu(ly_ref[n].astype(F32)) * h_scr[:, sl]).astype(BF16)


def _rg_lru(P, conv_w, conv_b, w_ax, b_a, b_x, lam, B, S, bt):
    T = B * S
    nt = S // bt
    W = LRU_BLOCKS * LRU_BW
    tok = lambda b, i: b * nt + i
    full = lambda shape: pl.BlockSpec(shape, lambda b, i: (0,) * len(shape))
    return pl.pallas_call(
        _lru_kernel,
        grid=(B, nt),
        in_specs=[
            pl.BlockSpec((8, bt, LANE), lambda b, i: (G_LX // 8, tok(b, i), 0)),
            pl.BlockSpec((8, bt, LANE), lambda b, i: (G_LY // 8, tok(b, i), 0)),
            full((CONV_W, W)), full((1, W)), full((LRU_BLOCKS, LRU_BW, 2 * LRU_BW)),
            full((1, W)), full((1, W)), full((1, W)),
        ],
        out_specs=pl.BlockSpec((bt, W), lambda b, i: (tok(b, i), 0)),
        out_shape=jax.ShapeDtypeStruct((T, W), BF16),
        scratch_shapes=[pltpu.VMEM((8, W), F32), pltpu.VMEM((8, W), F32),
                        pltpu.VMEM((bt, W), F32), pltpu.VMEM((bt, W), F32), pltpu.VMEM((bt, W), F32)],
        compiler_params=_cparams(("parallel", "arbitrary")),
        name="rg_lru",
    )(P, P, conv_w, conv_b.reshape(1, W), w_ax, b_a.reshape(1, W), b_x.reshape(1, W), lam.reshape(1, W))


def _cum_kernel(ff_ref, bf_ref, tri_ref, route_ref, const_ref, aq_ref, ak_ref, cl_ref, carry):
    bc = ff_ref.shape[0]

    @pl.when(pl.program_id(1) == 0)
    def _():
        carry[...] = jnp.zeros_like(carry)

    z = ff_ref[...] + bf_ref[...]
    logf = jnp.minimum(z, 0.0) - jnp.log1p(jnp.exp(-jnp.abs(z)))
    l1 = logf.astype(BF16)
    e1 = logf - l1.astype(F32)
    l2 = e1.astype(BF16)
    l3 = (e1 - l2.astype(F32)).astype(BF16)
    cs = jnp.dot(tri_ref[...], jnp.concatenate([l1, l2, l3], axis=1), preferred_element_type=F32)
    cum = (cs[:, :LANE] + cs[:, LANE:2 * LANE]) + cs[:, 2 * LANE:] + carry[...]
    carry[...] = cum[bc - 1:bc]
    cum = cum * LOG2E
    cl_ref[0] = cum[bc - 1:bc]
    hi = cum.astype(BF16).astype(F32)
    r1 = cum - hi
    mid = r1.astype(BF16).astype(F32)
    lo = r1 - mid
    lane = lax.broadcasted_iota(jnp.int32, (bc, LANE), 1)
    H = FOX_HEADS
    pieces = jnp.where(lane < H, hi, jnp.where(lane < 2 * H, pltpu.roll(mid, H, 1), pltpu.roll(lo, 2 * H, 1)))
    out = jnp.dot(pieces.astype(BF16), route_ref[...], preferred_element_type=F32) + const_ref[...]
    for h in range(H):
        aq_ref[h] = out[:, h * LANE:(h + 1) * LANE].astype(BF16)
        ak_ref[h] = out[:, (H + h) * LANE:(H + h + 1) * LANE].astype(BF16)


def _forget_cumsum(ff, b_f, B, S, tq, bc=512):
    T = B * S
    nt = S // bc
    bf = jnp.zeros((1, LANE), F32).at[0, :FOX_HEADS].set(b_f)
    tri = (jnp.arange(bc)[None, :] <= jnp.arange(bc)[:, None]).astype(BF16)
    H = FOX_HEADS
    hh, pp = jnp.meshgrid(jnp.arange(H), jnp.arange(3), indexing="ij")
    route = jnp.zeros((LANE, 2 * H, LANE), F32)
    route = route.at[pp * H + hh, hh, pp].set(1.0).at[pp * H + hh, H + hh, 3 + pp].set(-1.0)
    route = route.reshape(LANE, 2 * H * LANE).astype(BF16)
    const = jnp.zeros((2 * H, LANE), F32).at[:H, 3:6].set(1.0).at[H:, 0:3].set(1.0).reshape(1, 2 * H * LANE)
    tok = lambda b, i: b * nt + i
    out = jax.ShapeDtypeStruct((FOX_HEADS, T, LANE), BF16)
    aq, ak, cl = pl.pallas_call(
        _cum_kernel,
        grid=(B, nt),
        in_specs=[
            pl.BlockSpec((bc, LANE), lambda b, i: (tok(b, i), 0)),
            pl.BlockSpec((1, LANE), lambda b, i: (0, 0)),
            pl.BlockSpec((bc, bc), lambda b, i: (0, 0)),
            pl.BlockSpec((LANE, 2 * H * LANE), lambda b, i: (0, 0)),
            pl.BlockSpec((1, 2 * H * LANE), lambda b, i: (0, 0)),
        ],
        out_specs=[pl.BlockSpec((FOX_HEADS, bc, LANE), lambda b, i: (0, tok(b, i), 0))] * 2
        + [pl.BlockSpec((1, 1, LANE), lambda b, i: (tok(b, i), 0, 0))],
        out_shape=[out, out, jax.ShapeDtypeStruct((B * nt, 1, LANE), F32)],
        scratch_shapes=[pltpu.VMEM((1, LANE), F32)],
        compiler_params=_cparams(("parallel", "arbitrary")),
        name="forget_cumsum",
    )(ff, bf, tri, route, const)
    per = tq // bc
    nq = S // tq
    last = cl.reshape(B, nt, LANE)[:, per - 1::per, :FOX_HEADS]
    ncl = jnp.zeros((B, FOX_HEADS, LANE), F32).at[:, :, :nq].set(-last.transpose(0, 2, 1))
    return aq, ak, ncl.reshape(B * FOX_HEADS, 1, LANE)


FOX_UNROLL = 4
FOX_SUBSTEPS = 2
SUB = 256
BOUND_SLACK = 1.02
SKIP_MARGIN = 130.0


def _token_sums(w, x):
    return lax.dot_general(w, x, (((1,), (1,)), ((), ())), preferred_element_type=F32)


def _fox_block(i, n_blocks, q, aq, qn, aqn, out, ncl_ref, k_ref, ak_ref, v_ref, m_scr, acc_scr, kmax_scr,
               n_smem):
    tq = q.shape[0]
    token_sums = _token_sums
    ones_row = jnp.ones((8, LANE), BF16)
    qa = jnp.concatenate([q, aq], axis=-1)
    ones = jnp.ones((SUB, LANE), BF16)
    nsub = tq // SUB

    def scores(j):
        out = []
        for h in range(nsub):
            r0 = pl.multiple_of(j * tq + h * SUB, SUB)
            ka = jnp.concatenate([k_ref[0, pl.ds(r0, SUB), :], ak_ref[0, pl.ds(r0, SUB), :]], axis=-1)
            out.append(lax.dot_general(qa, ka, (((1,), (1,)), ((), ())), preferred_element_type=F32))
        return out

    def update(s, j, m_prev, acc):
        mx = jnp.max(s[0], axis=1, keepdims=True)
        for sh in s[1:]:
            mx = jnp.maximum(mx, jnp.max(sh, axis=1, keepdims=True))
        m_new = jnp.broadcast_to(mx, m_scr.shape) if m_prev is None else jnp.maximum(m_prev, mx)
        m_rep = jnp.concatenate([m_new] * (SUB // LANE), axis=1)
        pv = None
        for h, sh in enumerate(s):
            r0 = pl.multiple_of(j * tq + h * SUB, SUB)
            va = jnp.concatenate([v_ref[0, pl.ds(r0, SUB), :], ones], axis=-1)
            d = jnp.dot(jnp.exp2(sh - m_rep).astype(BF16), va, preferred_element_type=F32)
            pv = d if pv is None else pv + d
        if m_prev is not None:
            alpha = jnp.exp2(m_prev - m_new)
            pv = acc * jnp.concatenate([alpha, alpha], axis=-1) + pv
        return m_new, pv

    row = lax.broadcasted_iota(jnp.int32, (tq, SUB), 0)
    col = lax.broadcasted_iota(jnp.int32, (tq, SUB), 1)

    def run(js, first=False):
        s_list = [scores(j) for j in js]
        if first:
            s_list[0] = [jnp.where(col + h * SUB <= row, sh, NEG) for h, sh in enumerate(s_list[0])]
            m, acc = None, None
        else:
            m, acc = m_scr[...], acc_scr[...]
        for s, j in zip(s_list, js):
            m, acc = update(s, j, m, acc)
        m_scr[...] = m
        acc_scr[...] = acc

    def next_walk_length():
        i1 = jnp.minimum(i + 1, n_blocks - 1)
        kd = k_ref[0, pl.ds(pl.multiple_of(i1 * tq, tq), tq), :]
        wcol = lax.broadcasted_iota(jnp.int32, (8, 2 * LANE), 1)
        w_mc = jnp.where(wcol < LANE, 1.0, jnp.where(wcol < LANE + 3, -1.0, 0.0)).astype(BF16)
        mc = token_sums(w_mc, jnp.concatenate([qn * kd, aqn], axis=-1))
        qnorm = jnp.sqrt(token_sums(ones_row, qn * qn))
        r = mc - SKIP_MARGIN - BOUND_SLACK * qnorm * kmax_scr[:, 0:1]
        r_min = jnp.min(r, axis=1, keepdims=True)[0:1]
        lane = lax.broadcasted_iota(jnp.int32, (1, LANE), 1)
        keep = jnp.logical_and(lane < i1, ncl_ref[...] >= r_min)
        n_smem[0] = jnp.sum(jnp.where(keep, 1.0, 0.0)).astype(jnp.int32)

    n = jnp.where(i == 0, 0, n_smem[0])

    lead = jnp.minimum(n, FOX_UNROLL - 1)
    for size in range(FOX_UNROLL):

        @pl.when(lead == size)
        def _(size=size):
            next_walk_length()
            run([i - u for u in range(size + 1)], first=True)

    def group(t, carry):
        j0 = i - FOX_UNROLL - t * FOX_UNROLL
        run([j0 - u for u in range(FOX_UNROLL)])
        return carry

    ng = (n - lead) // FOX_UNROLL
    lax.fori_loop(0, ng, group, 0)
    done = lead + ng * FOX_UNROLL
    size = FOX_UNROLL // 2
    while size >= 1:
        take = ((n - done) & size) != 0
        j0 = i - 1 - done

        @pl.when(take)
        def _(j0=j0, size=size):
            run([j0 - u for u in range(size)])

        done = done + jnp.where(take, size, 0)
        size //= 2
    acc = acc_scr[...]
    out[...] = (acc[:, :LANE] / acc[:, LANE:]).astype(BF16)


def _fox_kernel(ncl_ref, q_ref, aq_ref, qx_ref, aqx_ref, k_ref, ak_ref, v_ref, o_ref,
                m_scr, acc_scr, kmax_scr, n_smem):
    tq = q_ref.shape[1] // FOX_SUBSTEPS
    S = k_ref.shape[1]
    step = pl.program_id(1)

    @pl.when(step == 0)
    def _():
        ones_row = jnp.ones((8, LANE), BF16)

        def body(c, mx):
            r0 = pl.multiple_of(c * tq, tq)
            kc = k_ref[0, pl.ds(r0, tq), :]
            return jnp.maximum(mx, _token_sums(ones_row, kc * kc))

        mx = lax.fori_loop(0, S // tq, body, jnp.zeros((8, tq), F32))
        kmax_scr[...] = jnp.broadcast_to(jnp.sqrt(jnp.max(mx, axis=1, keepdims=True)), kmax_scr.shape)

    for sb in range(FOX_SUBSTEPS):
        rows = slice(sb * tq, (sb + 1) * tq)
        if sb + 1 < FOX_SUBSTEPS:
            nrows = slice((sb + 1) * tq, (sb + 2) * tq)
            qn, aqn = q_ref[0, nrows, :], aq_ref[0, nrows, :]
        else:
            qn, aqn = qx_ref[0], aqx_ref[0]
        _fox_block(step * FOX_SUBSTEPS + sb, pl.num_programs(1) * FOX_SUBSTEPS, q_ref[0, rows, :],
                   aq_ref[0, rows, :], qn, aqn, o_ref.at[rows, :], ncl_ref, k_ref, ak_ref, v_ref,
                   m_scr, acc_scr, kmax_scr, n_smem)


def _fox_attention(P, Aq, Ak, ncl, B, S, tq):
    T = B * S
    nq = S // tq
    ns = nq // FOX_SUBSTEPS
    ts = tq * FOX_SUBSTEPS
    H = FOX_HEADS
    assert nq <= LANE and nq % FOX_SUBSTEPS == 0
    nxt = lambda i: jnp.minimum((i + 1) * FOX_SUBSTEPS, nq - 1)
    return pl.pallas_call(
        _fox_kernel,
        grid=(B * H, ns),
        in_specs=[
            pl.BlockSpec((None, 1, LANE), lambda g, i: (g, 0, 0)),
            pl.BlockSpec((1, ts, LANE), lambda g, i: (G_FQ + g % H, (g // H) * ns + i, 0)),
            pl.BlockSpec((1, ts, LANE), lambda g, i: (g % H, (g // H) * ns + i, 0)),
            pl.BlockSpec((1, tq, LANE), lambda g, i: (G_FQ + g % H, (g // H) * nq + nxt(i), 0)),
            pl.BlockSpec((1, tq, LANE), lambda g, i: (g % H, (g // H) * nq + nxt(i), 0)),
            pl.BlockSpec((1, S, LANE), lambda g, i: (G_FK + g % H, g // H, 0)),
            pl.BlockSpec((1, S, LANE), lambda g, i: (g % H, g // H, 0)),
            pl.BlockSpec((1, S, LANE), lambda g, i: (G_FV + g % H, g // H, 0)),
        ],
        out_specs=pl.BlockSpec((ts, LANE), lambda g, i: ((g // H) * ns + i, g % H)),
        out_shape=jax.ShapeDtypeStruct((T, H * FOX_DH), BF16),
        scratch_shapes=[pltpu.VMEM((tq, LANE), F32), pltpu.VMEM((tq, 2 * LANE), F32),
                        pltpu.VMEM((8, LANE), F32), pltpu.SMEM((1,), jnp.int32)],
        compiler_params=_cparams(("parallel", "arbitrary")),
        name="fox_attention",
    )(ncl, P, Aq, P, Aq, P, Ak, P)


def _merge_kernel(yr_ref, yl_ref, yf_ref, g0_ref, g1_ref, g2_ref, x_ref, mod_ref, gp_ref,
                  wr_ref, wl_ref, wf_ref, wo_ref, o_ref):
    D = D_MODEL

    def branch(y_ref, w_ref, g_ref):
        g = _sigmoid(_cat(g_ref, 0, 8).astype(F32))
        return g * jnp.dot(y_ref[...], w_ref[...], preferred_element_type=F32)

    m = branch(yr_ref, wr_ref, g0_ref) + branch(yl_ref, wl_ref, g1_ref) + branch(yf_ref, wf_ref, g2_ref)
    z = jnp.dot(m.astype(BF16), wo_ref[...], preferred_element_type=F32)
    o_ref[...] = x_ref[...] + mod_ref[:, 2 * D:3 * D] * _rms(z, gp_ref[...])


def _merge(y_ret, y_lru, y_fox, P, x2, mod_l, g_post, w_r, w_l, w_f, w_o, S, tm=512):
    T, D = x2.shape
    nt = S // tm
    row = pl.BlockSpec((tm, D), lambda i: (i, 0))
    wspec = pl.BlockSpec((D, D), lambda i: (0, 0))
    gate = lambda k: pl.BlockSpec((8, tm, LANE), lambda i: (G_GATE // 8 + k, i, 0))
    return pl.pallas_call(
        _merge_kernel,
        grid=(T // tm,),
        in_specs=[row, row, row, gate(0), gate(1), gate(2), row,
                  pl.BlockSpec((None, 1, 6 * D), lambda i: (i // nt, 0, 0)),
                  pl.BlockSpec((1, D), lambda i: (0, 0)),
                  wspec, wspec, wspec, wspec],
        out_specs=row,
        out_shape=jax.ShapeDtypeStruct((T, D), F32),
        compiler_params=_cparams(("parallel",)),
        name="merge_out_proj",
    )(y_ret, y_lru, y_fox, P, P, P, x2, mod_l, g_post.reshape(1, D), w_r, w_l, w_f, w_o)


def _mlp_kernel(x_ref, mod_ref, gpre_ref, gpost_ref, w1_ref, w2_ref, o_ref):
    D = D_MODEL
    x = x_ref[...]
    h = (_rms(x, gpre_ref[...]) * (1.0 + mod_ref[:, 4 * D:5 * D]) + mod_ref[:, 3 * D:4 * D]).astype(BF16)
    f = jnp.zeros(x.shape, F32)
    ck = 1024
    for c in range(D_FF // ck):
        a = jnp.maximum(jnp.dot(h, w1_ref[:, c * ck:(c + 1) * ck], preferred_element_type=F32), 0.0)
        f = f + jnp.dot((a * a).astype(BF16), w2_ref[c * ck:(c + 1) * ck, :], preferred_element_type=F32)
    o_ref[...] = x + mod_ref[:, 5 * D:6 * D] * _rms(f, gpost_ref[...])


def _mlp(x2, mod_l, g_pre, g_post, w1, w2, S, tm=512):
    T, D = x2.shape
    nt = S // tm
    row = pl.BlockSpec((tm, D), lambda i: (i, 0))
    vec = pl.BlockSpec((1, D), lambda i: (0, 0))
    return pl.pallas_call(
        _mlp_kernel,
        grid=(T // tm,),
        in_specs=[row, pl.BlockSpec((None, 1, 6 * D), lambda i: (i // nt, 0, 0)), vec, vec,
                  pl.BlockSpec((D, D_FF), lambda i: (0, 0)), pl.BlockSpec((D_FF, D), lambda i: (0, 0))],
        out_specs=row,
        out_shape=jax.ShapeDtypeStruct((T, D), F32),
        compiler_params=_cparams(("parallel",)),
        name="relu2_mlp",
    )(x2, mod_l, g_pre.reshape(1, D), g_post.reshape(1, D), w1, w2)


PREP_TN = 1024


def _prep_kernel(a_ref, b_ref, o_ref):
    j = pl.program_id(1)
    col = lax.broadcasted_iota(jnp.int32, (1, PREP_TN), 1)
    rk_block, rk_from = divmod(G_RK * LANE, PREP_TN)
    scale = jnp.where(jnp.logical_and(j == rk_block, col >= rk_from), RET_DK ** -0.5,
                      jnp.where(j == (G_FQ * LANE) // PREP_TN, FOX_DH ** -0.5 * LOG2E, 1.0))

    @pl.when(j < FF_COL // PREP_TN)
    def _():
        o_ref[0] = (a_ref[0] * scale).astype(BF16)

    @pl.when(j >= FF_COL // PREP_TN)
    def _():
        back = PREP_TN - FOX_HEADS
        moved = jnp.where(col < back, pltpu.roll(a_ref[0], back, 1), pltpu.roll(b_ref[0], back, 1))
        o_ref[0] = moved.astype(BF16)


def _prep_w_in(w_in):
    L, D, n_in = w_in.shape
    n_main = n_in - FOX_HEADS
    assert n_main % PREP_TN == 0 and FF_COL % PREP_TN == 0 and (G_FK - G_FQ) * LANE == PREP_TN
    assert (G_RK * LANE) % PREP_TN == PREP_TN // 2 and (G_RV - G_RK) * LANE == PREP_TN // 2
    blk = pl.BlockSpec((1, D, PREP_TN), lambda l, j: (l, 0, j))
    nxt = pl.BlockSpec((1, D, PREP_TN), lambda l, j: (l, 0, jnp.maximum(j, FF_COL // PREP_TN) + 1))
    w_main = pl.pallas_call(
        _prep_kernel,
        grid=(L, n_main // PREP_TN),
        in_specs=[blk, nxt],
        out_specs=blk,
        out_shape=jax.ShapeDtypeStruct((L, D, n_main), BF16),
        compiler_params=_cparams(("parallel", "parallel")),
        name="prep_w_in",
    )(w_in, w_in)
    w_ff = jnp.zeros((L, D, LANE), BF16).at[:, :, :FOX_HEADS].set(
        w_in[:, :, FF_COL:FF_COL + FOX_HEADS].astype(BF16))
    return w_main, w_ff


def kernel(x, c, ada_w, ada_b, norm_pre_mix, norm_post_mix, norm_pre_mlp, norm_post_mlp, w_in, conv_w, conv_b,
           lru_w_a, lru_b_a, lru_w_x, lru_b_x, lru_lambda, fox_b_f, ret_w_o, lru_w_o, fox_w_o, w_out, mlp_w1,
           mlp_w2):
    B, S, D = x.shape
    L = ada_w.shape[0]
    assert D == D_MODEL and S % 2048 == 0
    ret_bt, lru_bt, fox_tq = 256, 512, 512
    mod = _modulation(c, ada_w, ada_b)
    tables = _retention_tables(S, ret_bt)
    x2 = x.reshape(B * S, D)
    w_main, w_ff = _prep_w_in(w_in)
    for l in range(L):
        P, ff = _in_projection(x2, mod[l], norm_pre_mix[l], w_main, w_ff, l, S)
        y_ret = _retention(P, tables, B, S, ret_bt)
        w_ax = (jnp.concatenate([lru_w_a[l], lru_w_x[l]], axis=-1) * -LOG2E).astype(BF16)
        y_lru = _rg_lru(P, conv_w[l], conv_b[l], w_ax, lru_b_a[l] * -LOG2E, lru_b_x[l] * -LOG2E,
                        lru_lambda[l], B, S, lru_bt)
        Aq, Ak, ncl = _forget_cumsum(ff, fox_b_f[l], B, S, fox_tq)
        y_fox = _fox_attention(P, Aq, Ak, ncl, B, S, fox_tq)
        x2 = _merge(y_ret, y_lru, y_fox, P, x2, mod[l], norm_post_mix[l], ret_w_o[l].astype(BF16),
                    lru_w_o[l].astype(BF16), fox_w_o[l].astype(BF16), w_out[l].astype(BF16), S)
        x2 = _mlp(x2, mod[l], norm_pre_mlp[l], norm_post_mlp[l], mlp_w1[l].astype(BF16),
                  mlp_w2[l].astype(BF16), S)
    return x2.reshape(B, S, D)
```
